```python
import math
import jax, jax.numpy as jnp
from jax import lax
import numpy as np

D_MODEL = 2048
BATCH = 1
SEQ = 16384
DEPTH = 2

CHUNK = 64
SSM_WIDTH = D_MODEL // 2
SSM_GROUP = 16
SSM_GROUPS = SSM_WIDTH // SSM_GROUP
SSM_STATE = 64
SSM_DT_MIN = 0.001
SSM_DT_MAX = 0.1
GDN_HEADS = 8
GDN_DK = 128
GDN_DV = 128
GDN_KDIM = GDN_HEADS * GDN_DK
GDN_VDIM = GDN_HEADS * GDN_DV
GDN_CONV_CH = 2 * GDN_KDIM + GDN_VDIM
CONV_WIDTH = 4
GDN_DT_MIN = 0.001
GDN_DT_MAX = 0.1
GDN_A_MAX = 16.0
FFN_HIDDEN = -(-8 * D_MODEL // (3 * 256)) * 256
IN_SIZES = (SSM_WIDTH, GDN_KDIM, GDN_KDIM, GDN_VDIM, GDN_VDIM, GDN_HEADS, GDN_HEADS, D_MODEL, D_MODEL)
PROJ_IN = sum(IN_SIZES)
DEEPNORM_ALPHA = (2 * DEPTH) ** 0.25
DEEPNORM_BETA = (8 * DEPTH) ** -0.25
LN_EPS = 1e-5
NORM_EPS = 1e-6

kernel_name = 'hybrid_s5_gdn_deepnorm_adaln'


def layer_norm(x, gain=None, bias=None):
    xf = x.astype(jnp.float32)
    mu = xf.mean(-1, keepdims=True)
    var = jnp.square(xf - mu).mean(-1, keepdims=True)
    y = (xf - mu) * lax.rsqrt(var + LN_EPS)
    if gain is not None:
        y = y * gain.astype(jnp.float32) + bias.astype(jnp.float32)
    return y.astype(x.dtype)


def causal_dwconv(x, w):
    k = w.shape[0]
    xp = jnp.pad(x, ((0, 0), (k - 1, 0), (0, 0)))
    return lax.conv_general_dilated(xp, w.astype(x.dtype)[:, None, :], window_strides=(1,), padding='VALID',
                                    dimension_numbers=('NWC', 'WIO', 'NWC'), feature_group_count=x.shape[-1])


def s5_mixer(u, lam_re, lam_im, log_dt, b_re, b_im, c_re, c_im, d_skip, w_glu, b_glu):
    f32 = jnp.float32
    bsz, seq, _ = u.shape
    uf = u.astype(f32).reshape(bsz, seq, SSM_GROUPS, SSM_GROUP)
    lr, li = lam_re.astype(f32), lam_im.astype(f32)
    dt = jnp.exp(log_dt.astype(f32))[:, None]
    mag = jnp.exp(lr * dt)
    ang = li * dt
    ab_re, ab_im = mag * jnp.cos(ang), mag * jnp.sin(ang)
    den = lr * lr + li * li
    nr, ni = ab_re - 1.0, ab_im
    f_re = (nr * lr + ni * li) / den
    f_im = (ni * lr - nr * li) / den
    br, bi = b_re.astype(f32), b_im.astype(f32)
    bb_re = f_re[..., None] * br - f_im[..., None] * bi
    bb_im = f_re[..., None] * bi + f_im[..., None] * br
    bu_re = jnp.einsum('blgh,gph->blgp', uf, bb_re)
    bu_im = jnp.einsum('blgh,gph->blgp', uf, bb_im)
    a_re = jnp.broadcast_to(ab_re, bu_re.shape)
    a_im = jnp.broadcast_to(ab_im, bu_im.shape)

    def combine(e1, e2):
        a1r, a1i, b1r, b1i = e1
        a2r, a2i, b2r, b2i = e2
        return (a2r * a1r - a2i * a1i, a2r * a1i + a2i * a1r,
                a2r * b1r - a2i * b1i + b2r, a2r * b1i + a2i * b1r + b2i)

    _, _, xs_re, xs_im = lax.associative_scan(combine, (a_re, a_im, bu_re, bu_im), axis=1)
    y = (jnp.einsum('blgp,ghp->blgh', xs_re, c_re.astype(f32))
         - jnp.einsum('blgp,ghp->blgh', xs_im, c_im.astype(f32)))
    y = y.reshape(bsz, seq, SSM_WIDTH) + d_skip.astype(f32) * uf.reshape(bsz, seq, SSM_WIDTH)
    z = jax.nn.gelu(y)
    y = z * jax.nn.sigmoid(z @ w_glu.astype(f32) + b_glu.astype(f32))
    return y.astype(u.dtype)


def chunk_gated_delta_rule(q, k, v, beta, g):
    bsz, seq, nh, dk = q.shape
    dv = v.shape[-1]
    nc = seq // CHUNK

    def to_chunks(t):
        return t.reshape(bsz, nc, CHUNK, nh, -1).transpose(0, 3, 1, 2, 4)

    q, k, v = to_chunks(q), to_chunks(k), to_chunks(v)
    beta = beta.reshape(bsz, nc, CHUNK, nh).transpose(0, 3, 1, 2)
    g_cum = jnp.cumsum(g.reshape(bsz, nc, CHUNK, nh).transpose(0, 3, 1, 2), axis=-1)
    causal = jnp.tril(jnp.ones((CHUNK, CHUNK), dtype=bool))
    strict = jnp.tril(jnp.ones((CHUNK, CHUNK), dtype=bool), k=-1)
    diff = g_cum[..., :, None] - g_cum[..., None, :]
    decay = jnp.exp(jnp.where(causal, diff, -jnp.inf))
    k_beta = k * beta[..., None]
    m = jnp.where(strict, jnp.einsum('bhnid,bhnjd->bhnij', k_beta, k) * decay, 0.0)
    eye = jnp.eye(CHUNK, dtype=jnp.float32)
    t_inv = lax.linalg.triangular_solve(eye + m, jnp.broadcast_to(eye, m.shape), left_side=True,
                                        lower=True, unit_diagonal=True)
    u = jnp.einsum('bhnij,bhnjd->bhnid', t_inv, v * beta[..., None])
    w = jnp.einsum('bhnij,bhnjd->bhnid', t_inv, k_beta * jnp.exp(g_cum)[..., None])
    attn = jnp.einsum('bhnid,bhnjd->bhnij', q, k) * decay
    q_dec = q * jnp.exp(g_cum)[..., None]
    k_tail = k * jnp.exp(g_cum[..., -1:] - g_cum)[..., None]
    g_last = jnp.exp(g_cum[..., -1])

    def step(state, xs):
        u_c, w_c, attn_c, qd_c, kt_c, gl_c = xs
        v_new = u_c - jnp.einsum('bhcd,bhde->bhce', w_c, state)
        o_c = jnp.einsum('bhcd,bhde->bhce', qd_c, state) + jnp.einsum('bhij,bhje->bhie', attn_c, v_new)
        state = state * gl_c[..., None, None] + jnp.einsum('bhcd,bhce->bhde', kt_c, v_new)
        return state, o_c

    xs = tuple(jnp.moveaxis(t, 2, 0) for t in (u, w, attn, q_dec, k_tail, g_last))
    s0 = jnp.zeros((bsz, nh, dk, dv), jnp.float32)
    _, o = lax.scan(step, s0, xs)
    return o.transpose(1, 0, 3, 2, 4).reshape(bsz, seq, nh, dv)


def gated_deltanet(q, k, v, z, beta_logit, a_logit, conv_w, a_log, dt_bias, norm_w):
    f32 = jnp.float32
    bsz, seq, _ = q.shape
    qkv = jax.nn.silu(causal_dwconv(jnp.concatenate([q, k, v], axis=-1), conv_w)).astype(f32)
    q, k, v = jnp.split(qkv, [GDN_KDIM, 2 * GDN_KDIM], axis=-1)
    q = q.reshape(bsz, seq, GDN_HEADS, GDN_DK)
    k = k.reshape(bsz, seq, GDN_HEADS, GDN_DK)
    v = v.reshape(bsz, seq, GDN_HEADS, GDN_DV)
    q = q * lax.rsqrt(jnp.sum(q * q, -1, keepdims=True) + NORM_EPS) * (GDN_DK ** -0.5)
    k = k * lax.rsqrt(jnp.sum(k * k, -1, keepdims=True) + NORM_EPS)
    beta = jax.nn.sigmoid(beta_logit.astype(f32))
    g = -jnp.exp(a_log.astype(f32)) * jax.nn.softplus(a_logit.astype(f32) + dt_bias.astype(f32))
    o = chunk_gated_delta_rule(q, k, v, beta, g)
    o = o * lax.rsqrt(jnp.mean(o * o, -1, keepdims=True) + NORM_EPS) * norm_w.astype(f32)
    o = o * jax.nn.silu(z.astype(f32).reshape(bsz, seq, GDN_HEADS, GDN_DV))
    return o.reshape(bsz, seq, GDN_VDIM).astype(z.dtype)


def hybrid_mixer(h, w_in, lam_re, lam_im, log_dt, b_re, b_im, c_re, c_im, d_skip, w_glu, b_glu,
                 conv_w, a_log, dt_bias, norm_w, w_up_ssm, w_up_gdn, w_out):
    splits = np.cumsum(IN_SIZES)[:-1].tolist()
    u, q, k, v, z, beta_l, a_l, gate_s, gate_g = jnp.split(h @ w_in, splits, axis=-1)
    y_s = s5_mixer(u, lam_re, lam_im, log_dt, b_re, b_im, c_re, c_im, d_skip, w_glu, b_glu) @ w_up_ssm
    y_g = gated_deltanet(q, k, v, z, beta_l, a_l, conv_w, a_log, dt_bias, norm_w) @ w_up_gdn
    merged = jax.nn.sigmoid(gate_s) * y_s + jax.nn.sigmoid(gate_g) * y_g
    return merged @ w_out


def swiglu(h, w_in, w_out):
    gate, up = jnp.split(h @ w_in, 2, axis=-1)
    return (jax.nn.silu(gate) * up) @ w_out


def setup_inputs(seed: int = 0) -> dict:
    key = jax.random.key(seed)
    keys = iter(jax.random.split(key, 40))
    f32 = jnp.float32
    nl = DEPTH

    def normal(shape, scale):
        return scale * jax.random.normal(next(keys), shape, f32)

    def uniform(shape, lo, hi):
        return jax.random.uniform(next(keys), shape, f32, lo, hi)

    x = normal((BATCH, SEQ, D_MODEL), 1.0)
    c = normal((BATCH, D_MODEL), 1.0)
    w_ada = normal((nl, D_MODEL, 6 * D_MODEL), D_MODEL ** -0.5)
    b_ada = normal((nl, 6 * D_MODEL), 0.02)
    w_in = normal((nl, D_MODEL, PROJ_IN), D_MODEL ** -0.5)
    n_idx = jnp.arange(SSM_STATE, dtype=f32)
    ssm_lam_re = -0.5 + normal((nl, SSM_GROUPS, SSM_STATE), 0.01)
    ssm_lam_im = math.pi * n_idx + normal((nl, SSM_GROUPS, SSM_STATE), 0.01)
    ssm_log_dt = uniform((nl, SSM_GROUPS), math.log(SSM_DT_MIN), math.log(SSM_DT_MAX))
    ssm_b_re = normal((nl, SSM_GROUPS, SSM_STATE, SSM_GROUP), (2 * SSM_GROUP) ** -0.5)
    ssm_b_im = normal((nl, SSM_GROUPS, SSM_STATE, SSM_GROUP), (2 * SSM_GROUP) ** -0.5)
    ssm_c_re = normal((nl, SSM_GROUPS, SSM_GROUP, SSM_STATE), SSM_STATE ** -0.5)
    ssm_c_im = normal((nl, SSM_GROUPS, SSM_GROUP, SSM_STATE), SSM_STATE ** -0.5)
    ssm_d = normal((nl, SSM_WIDTH), 1.0)
    ssm_w_glu = normal((nl, SSM_WIDTH, SSM_WIDTH), SSM_WIDTH ** -0.5)
    ssm_b_glu = normal((nl, SSM_WIDTH), 0.02)
    gdn_conv_w = normal((nl, CONV_WIDTH, GDN_CONV_CH), CONV_WIDTH ** -0.5)
    gdn_a_log = jnp.log(uniform((nl, GDN_HEADS), 1.0, GDN_A_MAX))
    dt = jnp.exp(uniform((nl, GDN_HEADS), math.log(GDN_DT_MIN), math.log(GDN_DT_MAX)))
    gdn_dt_bias = dt + jnp.log(-jnp.expm1(-dt))
    gdn_norm_w = 1.0 + normal((nl, GDN_DV), 0.02)
    w_up_ssm = normal((nl, SSM_WIDTH, D_MODEL), SSM_WIDTH ** -0.5)
    w_up_gdn = normal((nl, GDN_VDIM, D_MODEL), GDN_VDIM ** -0.5)
    w_mix_out = normal((nl, D_MODEL, D_MODEL), DEEPNORM_BETA * D_MODEL ** -0.5)
    ln1_g = 1.0 + normal((nl, D_MODEL), 0.02)
    ln1_b = normal((nl, D_MODEL), 0.02)
    ffn_w_in = normal((nl, D_MODEL, 2 * FFN_HIDDEN), D_MODEL ** -0.5)
    ffn_w_out = normal((nl, FFN_HIDDEN, D_MODEL), DEEPNORM_BETA * FFN_HIDDEN ** -0.5)
    ln2_g = 1.0 + normal((nl, D_MODEL), 0.02)
    ln2_b = normal((nl, D_MODEL), 0.02)
    return {'x': x, 'c': c, 'w_ada': w_ada, 'b_ada': b_ada, 'w_in': w_in,
            'ssm_lam_re': ssm_lam_re, 'ssm_lam_im': ssm_lam_im, 'ssm_log_dt': ssm_log_dt,
            'ssm_b_re': ssm_b_re, 'ssm_b_im': ssm_b_im, 'ssm_c_re': ssm_c_re, 'ssm_c_im': ssm_c_im,
            'ssm_d': ssm_d, 'ssm_w_glu': ssm_w_glu, 'ssm_b_glu': ssm_b_glu,
            'gdn_conv_w': gdn_conv_w, 'gdn_a_log': gdn_a_log, 'gdn_dt_bias': gdn_dt_bias, 'gdn_norm_w': gdn_norm_w,
            'w_up_ssm': w_up_ssm, 'w_up_gdn': w_up_gdn, 'w_mix_out': w_mix_out,
            'ln1_g': ln1_g, 'ln1_b': ln1_b, 'ffn_w_in': ffn_w_in, 'ffn_w_out': ffn_w_out,
            'ln2_g': ln2_g, 'ln2_b': ln2_b}


def reference(x, c, w_ada, b_ada, w_in, ssm_lam_re, ssm_lam_im, ssm_log_dt, ssm_b_re, ssm_b_im,
              ssm_c_re, ssm_c_im, ssm_d, ssm_w_glu, ssm_b_glu, gdn_conv_w, gdn_a_log, gdn_dt_bias,
              gdn_norm_w, w_up_ssm, w_up_gdn, w_mix_out, ln1_g, ln1_b, ffn_w_in, ffn_w_out, ln2_g, ln2_b):
    for l in range(DEPTH):
        mod = (jax.nn.silu(c) @ w_ada[l] + b_ada[l])[:, None, :]
        sh_m, sc_m, g_m, sh_f, sc_f, g_f = jnp.split(mod, 6, axis=-1)
        h = layer_norm(x) * (1 + sc_m) + sh_m
        y = hybrid_mixer(h, w_in[l], ssm_lam_re[l], ssm_lam_im[l], ssm_log_dt[l], ssm_b_re[l], ssm_b_im[l],
                         ssm_c_re[l], ssm_c_im[l], ssm_d[l], ssm_w_glu[l], ssm_b_glu[l],
                         gdn_conv_w[l], gdn_a_log[l], gdn_dt_bias[l], gdn_norm_w[l],
                         w_up_ssm[l], w_up_gdn[l], w_mix_out[l])
        x = layer_norm(DEEPNORM_ALPHA * x + g_m * y, ln1_g[l], ln1_b[l])
        h = layer_norm(x) * (1 + sc_f) + sh_f
        x = layer_norm(DEEPNORM_ALPHA * x + g_f * swiglu(h, ffn_w_in[l], ffn_w_out[l]), ln2_g[l], ln2_b[l])
    return x
```

```python
import functools
import math

import jax
import jax.numpy as jnp
from jax import lax
from jax.experimental import pallas as pl
from jax.experimental.pallas import tpu as pltpu

F32 = jnp.float32
BF16 = jnp.bfloat16

D_MODEL = 2048
DEPTH = 2
CHUNK = 64
SSM_WIDTH = D_MODEL // 2
SSM_GROUP = 16
SSM_GROUPS = SSM_WIDTH // SSM_GROUP
SSM_STATE = 64
GDN_HEADS = 8
GDN_DK = 128
GDN_DV = 128
GDN_KDIM = GDN_HEADS * GDN_DK
GDN_VDIM = GDN_HEADS * GDN_DV
CONV_WIDTH = 4
FFN_HIDDEN = -(-8 * D_MODEL // (3 * 256)) * 256
DEEPNORM_ALPHA = (2 * DEPTH) ** 0.25
LN_EPS = 1e-5
NORM_EPS = 1e-6

SUBLANES = 8
LANES = 128
VMEM_LIMIT_BYTES = 56 * 1024 * 1024

COL_GATE_S = 0
COL_GATE_G = D_MODEL
COL_U = 2 * D_MODEL
COL_Q = COL_U + SSM_WIDTH
COL_K = COL_Q + GDN_KDIM
COL_V = COL_K + GDN_KDIM
COL_Z = COL_V + GDN_VDIM
PROJ_MAIN = COL_Z + GDN_VDIM
SMALL_ROWS = 2 * GDN_HEADS

S5_SEG = 64
S5_BLOCK = SUBLANES * S5_SEG
S5_CLUSTERS = SSM_WIDTH // LANES
S5_CSTATE = (LANES // SSM_GROUP) * SSM_STATE

GDN_BLOCK = 256


def _params(sem):
    return pltpu.CompilerParams(dimension_semantics=sem, vmem_limit_bytes=VMEM_LIMIT_BYTES)


def _ln(x):
    mu = jnp.mean(x, axis=-1, keepdims=True)
    xc = x - mu
    var = jnp.mean(xc * xc, axis=-1, keepdims=True)
    return xc * lax.rsqrt(var + LN_EPS)


def _sigmoid(x):
    return 1.0 / (1.0 + jnp.exp(-x))


def _silu(x):
    return x * _sigmoid(x)


def _softplus(x):
    return jnp.maximum(x, 0.0) + jnp.log(1.0 + jnp.exp(-jnp.abs(x)))


def _dot(a, b):
    return jnp.dot(a, b, preferred_element_type=F32)


def _dot_nt(a, b):
    return lax.dot_general(a, b, (((1,), (1,)), ((), ())), preferred_element_type=F32)


def _ada_kernel(c_ref, w_ref, b_ref, o_ref):
    c = c_ref[...]
    s = _silu(c).astype(BF16)
    o_ref[0] = _dot(s, w_ref[0].astype(BF16)) + b_ref[0]


def _ada_call(c, w_ada, b_ada):
    nl, d, n = w_ada.shape
    tn = 1024
    c8 = jnp.broadcast_to(c, (SUBLANES, d))
    out = pl.pallas_call(
        _ada_kernel,
        grid=(nl, n // tn),
        in_specs=[pl.BlockSpec((SUBLANES, d), lambda l, j: (0, 0)),
                  pl.BlockSpec((1, d, tn), lambda l, j: (l, 0, j)),
                  pl.BlockSpec((1, 1, tn), lambda l, j: (l, 0, j))],
        out_specs=pl.BlockSpec((1, SUBLANES, tn), lambda l, j: (l, 0, j)),
        out_shape=jax.ShapeDtypeStruct((nl, SUBLANES, n), F32),
        compiler_params=_params(("arbitrary", "arbitrary")),
        name="adaln_mod",
    )(c8, w_ada, b_ada.reshape(nl, 1, n))
    return out[:, 0:1, :]


def _lnmod_kernel(x_ref, sc_ref, sh_ref, h_ref):
    y = _ln(x_ref[...])
    h_ref[...] = (y * (1.0 + sc_ref[...]) + sh_ref[...]).astype(BF16)


def _lnmod_call(x, sc, sh):
    l, d = x.shape
    tm = 512
    return pl.pallas_call(
        _lnmod_kernel,
        grid=(l // tm,),
        in_specs=[pl.BlockSpec((tm, d), lambda i: (i, 0)),
                  pl.BlockSpec((1, d), lambda i: (0, 0)),
                  pl.BlockSpec((1, d), lambda i: (0, 0))],
        out_specs=pl.BlockSpec((tm, d), lambda i: (i, 0)),
        out_shape=jax.ShapeDtypeStruct((l, d), BF16),
        compiler_params=_params(("arbitrary",)),
        name="ln0_mod",
    )(x, sc, sh)


def _proj_kernel(h_ref, wm_ref, ws_ref, wst_ref, pm_ref, ps_ref, pst_ref):
    h = h_ref[...]
    pm_ref[...] = _dot(h, wm_ref[...])

    @pl.when(pl.program_id(1) == 0)
    def _():
        ps_ref[...] = _dot(h, ws_ref[...])
        pst_ref[...] = _dot_nt(wst_ref[...], h)


def _proj_call(h, w_main, w_small, w_small_t):
    l, d = h.shape
    n = w_main.shape[1]
    tm, tn = 1024, 1024
    return pl.pallas_call(
        _proj_kernel,
        grid=(l // tm, n // tn),
        in_specs=[pl.BlockSpec((tm, d), lambda i, j: (i, 0)),
                  pl.BlockSpec((d, tn), lambda i, j: (0, j)),
                  pl.BlockSpec((d, LANES), lambda i, j: (0, 0)),
                  pl.BlockSpec((SMALL_ROWS, d), lambda i, j: (0, 0))],
        out_specs=[pl.BlockSpec((tm, tn), lambda i, j: (i, j)),
                   pl.BlockSpec((tm, LANES), lambda i, j: (i, 0)),
                   pl.BlockSpec((SMALL_ROWS, tm), lambda i, j: (0, i))],
        out_shape=[jax.ShapeDtypeStruct((l, n), F32),
                   jax.ShapeDtypeStruct((l, LANES), F32),
                   jax.ShapeDtypeStruct((SMALL_ROWS, l), F32)],
        compiler_params=_params(("arbitrary", "arbitrary")),
        name="in_proj",
    )(h, w_main, w_small, w_small_t)


def _s5_kernel(*refs):
    u_refs = refs[:S5_CLUSTERS]
    (bmat_ref, cmat_ref, a_ref, aseg_ref, dskip_ref, wglu_ref, bglu_ref, o_ref,
     bu_ref, z_ref, carry_ref, outp_ref) = refs[S5_CLUSTERS:]

    @pl.when(pl.program_id(0) == 0)
    def _():
        carry_ref[...] = jnp.zeros_like(carry_ref)

    sub = lax.broadcasted_iota(jnp.int32, (SUBLANES, S5_CSTATE), 0)
    half = S5_CSTATE

    for cg in range(S5_CLUSTERS):
        cols = slice(cg * LANES, (cg + 1) * LANES)
        ucl = jnp.concatenate(
            [u_refs[cg][pl.ds(i, SUBLANES, stride=S5_SEG), :] for i in range(S5_SEG)], axis=0)
        bu_ref[...] = _dot(ucl.astype(BF16), bmat_ref[cg])
        ar = jnp.broadcast_to(a_ref[cg, 0:1, :], (SUBLANES, half))
        ai = jnp.broadcast_to(a_ref[cg, 1:2, :], (SUBLANES, half))

        def load_bu(i):
            r0 = pl.multiple_of(i * SUBLANES, SUBLANES)
            return bu_ref[pl.ds(r0, SUBLANES), 0:half], bu_ref[pl.ds(r0, SUBLANES), half:2 * half]

        def end_state_step(i, carry):
            xr, xi = carry
            bur, bui = load_bu(i)
            return ar * xr - ai * xi + bur, ar * xi + ai * xr + bui

        zero = jnp.zeros((SUBLANES, half), F32)
        er, ei = lax.fori_loop(0, S5_SEG, end_state_step, (zero, zero), unroll=8)

        pr = jnp.broadcast_to(aseg_ref[cg, 0:1, :], (SUBLANES, half))
        pi = jnp.broadcast_to(aseg_ref[cg, 1:2, :], (SUBLANES, half))
        cr = carry_ref[cg, :, 0:half]
        ci = carry_ref[cg, :, half:2 * half]
        sr, si = zero, zero
        for s in range(SUBLANES):
            sr = jnp.where(sub == s, cr, sr)
            si = jnp.where(sub == s, ci, si)
            esr = jnp.broadcast_to(er[s:s + 1, :], (SUBLANES, half))
            esi = jnp.broadcast_to(ei[s:s + 1, :], (SUBLANES, half))
            cr, ci = pr * cr - pi * ci + esr, pr * ci + pi * cr + esi
        carry_ref[cg, :, 0:half] = cr
        carry_ref[cg, :, half:2 * half] = ci

        def state_step(i, carry):
            xr, xi = carry
            bur, bui = load_bu(i)
            nxr = ar * xr - ai * xi + bur
            nxi = ar * xi + ai * xr + bui
            r0 = pl.multiple_of(i * SUBLANES, SUBLANES)
            bu_ref[pl.ds(r0, SUBLANES), 0:half] = nxr
            bu_ref[pl.ds(r0, SUBLANES), half:2 * half] = nxi
            return nxr, nxi

        lax.fori_loop(0, S5_SEG, state_step, (sr, si), unroll=8)

        y = _dot(bu_ref[...].astype(BF16), cmat_ref[cg]) + dskip_ref[:, cols] * ucl
        z_ref[:, cols] = jax.nn.gelu(y, approximate=True)

    z = z_ref[...]
    gate = _dot(z.astype(BF16), wglu_ref[...]) + bglu_ref[...]
    outp = z * _sigmoid(gate)
    for cg in range(S5_CLUSTERS):
        cols = slice(cg * LANES, (cg + 1) * LANES)
        for i in range(S5_SEG):
            outp_ref[cg, pl.ds(i, SUBLANES, stride=S5_SEG), :] = outp[i * SUBLANES:(i + 1) * SUBLANES, cols]
        o_ref[:, cols] = outp_ref[cg].astype(BF16)


def _s5_call(p_main, bmat, cmat, a_tab, aseg_tab, d_skip, w_glu, b_glu):
    l = p_main.shape[0]
    w = SSM_WIDTH
    full = lambda *shape: pl.BlockSpec(shape, lambda b: (0,) * len(shape))
    return pl.pallas_call(
        _s5_kernel,
        grid=(l // S5_BLOCK,),
        in_specs=[pl.BlockSpec((S5_BLOCK, LANES), functools.partial(lambda cg, b: (b, COL_U // LANES + cg), cg))
                  for cg in range(S5_CLUSTERS)] + [
                  full(S5_CLUSTERS, LANES, 2 * S5_CSTATE),
                  full(S5_CLUSTERS, 2 * S5_CSTATE, LANES),
                  full(S5_CLUSTERS, 2, S5_CSTATE),
                  full(S5_CLUSTERS, 2, S5_CSTATE),
                  full(1, w), full(w, w), full(1, w)],
        out_specs=pl.BlockSpec((S5_BLOCK, w), lambda b: (b, 0)),
        out_shape=jax.ShapeDtypeStruct((l, w), BF16),
        scratch_shapes=[pltpu.VMEM((S5_BLOCK, 2 * S5_CSTATE), F32),
                        pltpu.VMEM((S5_BLOCK, w), F32),
                        pltpu.VMEM((S5_CLUSTERS, SUBLANES, 2 * S5_CSTATE), F32),
                        pltpu.VMEM((S5_CLUSTERS, S5_BLOCK, LANES), F32)],
        compiler_params=_params(("arbitrary",)),
        name="s5_mixer",
    )(*([p_main] * S5_CLUSTERS), bmat, cmat, a_tab, aseg_tab, d_skip, w_glu, b_glu)


def _s5_tables(lam_re, lam_im, log_dt, b_re, b_im, c_re, c_im):
    lr, li = lam_re.astype(F32), lam_im.astype(F32)
    dt = jnp.exp(log_dt.astype(F32))[:, None]
    mag = jnp.exp(lr * dt)
    ang = li * dt
    ab_re, ab_im = mag * jnp.cos(ang), mag * jnp.sin(ang)
    den = lr * lr + li * li
    nr, ni = ab_re - 1.0, ab_im
    f_re = (nr * lr + ni * li) / den
    f_im = (ni * lr - nr * li) / den
    br, bi = b_re.astype(F32), b_im.astype(F32)
    bb_re = f_re[..., None] * br - f_im[..., None] * bi
    bb_im = f_re[..., None] * bi + f_im[..., None] * br
    gpc = LANES // SSM_GROUP
    eye = jnp.eye(gpc, dtype=F32)

    def in_block(bb):
        t = bb.reshape(S5_CLUSTERS, gpc, SSM_STATE, SSM_GROUP)
        m = jnp.einsum('cgph,gk->cghkp', t, eye)
        return m.reshape(S5_CLUSTERS, gpc * SSM_GROUP, gpc * SSM_STATE)

    def out_block(cc):
        t = cc.reshape(S5_CLUSTERS, gpc, SSM_GROUP, SSM_STATE)
        m = jnp.einsum('cghp,gk->cgpkh', t, eye)
        return m.reshape(S5_CLUSTERS, gpc * SSM_STATE, gpc * SSM_GROUP)

    bmat = jnp.concatenate([in_block(bb_re), in_block(bb_im)], axis=-1).astype(BF16)
    cmat = jnp.concatenate([out_block(c_re.astype(F32)), -out_block(c_im.astype(F32))], axis=1).astype(BF16)

    def tab(re, im):
        return jnp.stack([re.reshape(S5_CLUSTERS, S5_CSTATE), im.reshape(S5_CLUSTERS, S5_CSTATE)], axis=1)

    a_tab = tab(ab_re, ab_im)
    pr, pi = ab_re, ab_im
    for _ in range(int(math.log2(S5_SEG))):
        pr, pi = pr * pr - pi * pi, 2.0 * pr * pi
    return bmat, cmat, a_tab, tab(pr, pi)


def _tri_inverse(m, row, col):
    n = -m
    blk = 16

    def block_id(idx, size):
        return jnp.right_shift(idx, int(math.log2(size)))

    same = block_id(row, blk) == block_id(col, blk)
    eye = jnp.where(row == col, 1.0, 0.0)
    nd = jnp.where(same, n, 0.0)
    x = eye + nd
    p = nd
    for _ in range(3):
        pb = p.astype(BF16)
        p = _dot(pb, pb)
        x = x + _dot(p.astype(BF16), x.astype(BF16))
    while blk < CHUNK:
        c = jnp.where(block_id(row, 2 * blk) == block_id(col, 2 * blk),
                      jnp.where(block_id(row, blk) != block_id(col, blk), m, 0.0), 0.0)
        xb = x.astype(BF16)
        x = x - _dot(_dot(xb, c.astype(BF16)).astype(BF16), xb)
        blk *= 2
    return x


def _gdn_kernel(q_ref, k_ref, v_ref, z_ref, ps_ref, pst_ref, cwq_ref, cwk_ref, cwv_ref,
                rowp_ref, colp_ref, nw_ref, o_ref, s_ref, tq_ref, tk_ref, tv_ref):
    h = pl.program_id(0)
    r = GDN_BLOCK

    @pl.when(pl.program_id(1) == 0)
    def _():
        s_ref[...] = jnp.zeros_like(s_ref)
        tq_ref[...] = jnp.zeros_like(tq_ref)
        tk_ref[...] = jnp.zeros_like(tk_ref)
        tv_ref[...] = jnp.zeros_like(tv_ref)

    def conv_silu(x_ref, tail_ref, w_ref):
        x = x_ref[...]
        ext = jnp.concatenate([tail_ref[...], x], axis=0)
        w = w_ref[...]
        y = w[CONV_WIDTH - 1:CONV_WIDTH, :] * x
        for j in range(1, CONV_WIDTH):
            y = y + w[CONV_WIDTH - 1 - j:CONV_WIDTH - j, :] * ext[SUBLANES - j:SUBLANES - j + r, :]
        tail_ref[...] = x[r - SUBLANES:r, :]
        return _silu(y)

    q = conv_silu(q_ref, tq_ref, cwq_ref)
    k = conv_silu(k_ref, tk_ref, cwk_ref)
    v = conv_silu(v_ref, tv_ref, cwv_ref)
    q = q * lax.rsqrt(jnp.sum(q * q, axis=-1, keepdims=True) + NORM_EPS) * (GDN_DK ** -0.5)
    k = k * lax.rsqrt(jnp.sum(k * k, axis=-1, keepdims=True) + NORM_EPS)

    ps = ps_ref[...]
    lane = lax.broadcasted_iota(jnp.int32, (r, LANES), 1)
    rowi = lax.broadcasted_iota(jnp.int32, (r, LANES), 0)
    g_all = rowp_ref[0:1, :] * _softplus(ps + rowp_ref[1:2, :])
    for sh in (1, 2, 4, 8, 16, 32):
        g_all = g_all + jnp.where((rowi & (CHUNK - 1)) >= sh, pltpu.roll(g_all, sh, 0), 0.0)
    beta = jnp.sum(jnp.where(lane == h, _sigmoid(ps), 0.0), axis=1, keepdims=True)
    gc_col = jnp.sum(jnp.where(lane == h + GDN_HEADS, g_all, 0.0), axis=1, keepdims=True)

    pst = pst_ref[...]
    lane_t = lax.broadcasted_iota(jnp.int32, (SMALL_ROWS, r), 1)
    sub_t = lax.broadcasted_iota(jnp.int32, (SMALL_ROWS, r), 0)
    g_t = colp_ref[:, 0:1] * _softplus(pst + colp_ref[:, 1:2])
    for sh in (1, 2, 4, 8, 16, 32):
        g_t = g_t + jnp.where((lane_t & (CHUNK - 1)) >= sh, pltpu.roll(g_t, sh, 1), 0.0)
    gc_row = jnp.sum(jnp.where(sub_t == h + GDN_HEADS, g_t, 0.0), axis=0, keepdims=True)

    row = lax.broadcasted_iota(jnp.int32, (CHUNK, CHUNK), 0)
    col = lax.broadcasted_iota(jnp.int32, (CHUNK, CHUNK), 1)
    causal = row >= col
    strict = row > col

    state = s_ref[...]
    for c in range(r // CHUNK):
        rows = slice(c * CHUNK, (c + 1) * CHUNK)
        qc, kc, vc = q[rows], k[rows], v[rows]
        bc = beta[rows]
        gcc = gc_col[rows]
        gcr = gc_row[:, c * CHUNK:(c + 1) * CHUNK]
        g_end = gcc[CHUNK - 1:CHUNK, :]
        eg = jnp.exp(gcc)
        decay = jnp.where(causal, jnp.exp(jnp.where(causal, gcc - gcr, 0.0)), 0.0)
        kb = kc * bc
        kq = _dot_nt(jnp.concatenate([kb, qc], axis=0).astype(BF16), kc.astype(BF16))
        m = jnp.where(strict, kq[:CHUNK] * decay, 0.0)
        attn = kq[CHUNK:] * decay
        t_inv = _tri_inverse(m, row, col)
        uw = _dot(t_inv.astype(BF16), jnp.concatenate([vc * bc, kb * eg], axis=1).astype(BF16))
        u_c, w_c = uw[:, :GDN_DV], uw[:, GDN_DV:]
        qd = qc * eg
        kt = kc * jnp.exp(g_end - gcc)
        wq = _dot(jnp.concatenate([w_c, qd], axis=0).astype(BF16), state.astype(BF16))
        v_new = u_c - wq[:CHUNK]
        o_c = wq[CHUNK:] + _dot(attn.astype(BF16), v_new.astype(BF16))
        state = state * jnp.exp(g_end) + lax.dot_general(
            kt.astype(BF16), v_new.astype(BF16), (((0,), (0,)), ((), ())), preferred_element_type=F32)
        o_n = o_c * lax.rsqrt(jnp.mean(o_c * o_c, axis=-1, keepdims=True) + NORM_EPS) * nw_ref[...]
        o_ref[rows, :] = (o_n * _silu(z_ref[rows, :])).astype(BF16)
    s_ref[...] = state


def _gdn_call(p_main, p_small, p_small_t, conv_w, rowp, colp, norm_w):
    l = p_main.shape[0]
    r = GDN_BLOCK
    hb = lambda base: (lambda h, b: (b, base // LANES + h))
    cw = lambda base: pl.BlockSpec((CONV_WIDTH, LANES), lambda h, b: (0, base // LANES + h))
    return pl.pallas_call(
        _gdn_kernel,
        grid=(GDN_HEADS, l // r),
        in_specs=[pl.BlockSpec((r, LANES), hb(COL_Q)),
                  pl.BlockSpec((r, LANES), hb(COL_K)),
                  pl.BlockSpec((r, LANES), hb(COL_V)),
                  pl.BlockSpec((r, LANES), hb(COL_Z)),
                  pl.BlockSpec((r, LANES), lambda h, b: (b, 0)),
                  pl.BlockSpec((SMALL_ROWS, r), lambda h, b: (0, b)),
                  cw(0), cw(GDN_KDIM), cw(2 * GDN_KDIM),
                  pl.BlockSpec((2, LANES), lambda h, b: (0, 0)),
                  pl.BlockSpec((SMALL_ROWS, 2), lambda h, b: (0, 0)),
                  pl.BlockSpec((1, GDN_DV), lambda h, b: (0, 0))],
        out_specs=pl.BlockSpec((r, LANES), lambda h, b: (b, h)),
        out_shape=jax.ShapeDtypeStruct((l, GDN_VDIM), BF16),
        scratch_shapes=[pltpu.VMEM((GDN_DK, GDN_DV), F32),
                        pltpu.VMEM((SUBLANES, LANES), F32),
                        pltpu.VMEM((SUBLANES, LANES), F32),
                        pltpu.VMEM((SUBLANES, LANES), F32)],
        compiler_params=_params(("arbitrary", "arbitrary")),
        name="gated_deltanet",
    )(p_main, p_main, p_main, p_main, p_small, p_small_t, conv_w, conv_w, conv_w, rowp, colp, norm_w)


def _merge_kernel(ys_ref, yg_ref, gs_ref, gg_ref, x_ref, wus_ref, wug_ref, wo_ref,
                  gm_ref, g1_ref, b1_ref, sc_ref, sh_ref, xo_ref, ho_ref):
    a = _dot(ys_ref[...], wus_ref[...])
    b = _dot(yg_ref[...], wug_ref[...])
    m = _sigmoid(gs_ref[...]) * a + _sigmoid(gg_ref[...]) * b
    y = _dot(m.astype(BF16), wo_ref[...])
    xn = _ln(DEEPNORM_ALPHA * x_ref[...] + gm_ref[...] * y) * g1_ref[...] + b1_ref[...]
    xo_ref[...] = xn
    ho_ref[...] = (_ln(xn) * (1.0 + sc_ref[...]) + sh_ref[...]).astype(BF16)


def _merge_call(ys, yg, p_main, x, w_us, w_ug, w_out, g_m, ln_g, ln_b, sc_f, sh_f):
    l, d = x.shape
    tm = 256
    vec = pl.BlockSpec((1, d), lambda i: (0, 0))
    once = lambda shape: pl.BlockSpec(shape, lambda i: (0, 0), pipeline_mode=pl.Buffered(1))
    return pl.pallas_call(
        _merge_kernel,
        grid=(l // tm,),
        in_specs=[pl.BlockSpec((tm, SSM_WIDTH), lambda i: (i, 0)),
                  pl.BlockSpec((tm, GDN_VDIM), lambda i: (i, 0)),
                  pl.BlockSpec((tm, d), lambda i: (i, COL_GATE_S // d)),
                  pl.BlockSpec((tm, d), lambda i: (i, COL_GATE_G // d)),
                  pl.BlockSpec((tm, d), lambda i: (i, 0)),
                  once((SSM_WIDTH, d)), once((GDN_VDIM, d)), once((d, d)),
                  vec, vec, vec, vec, vec],
        out_specs=[pl.BlockSpec((tm, d), lambda i: (i, 0)),
                   pl.BlockSpec((tm, d), lambda i: (i, 0))],
        out_shape=[jax.ShapeDtypeStruct((l, d), F32), jax.ShapeDtypeStruct((l, d), BF16)],
        compiler_params=_params(("arbitrary",)),
        name="merge_out_ln1",
    )(ys, yg, p_main, p_main, x, w_us, w_ug, w_out, g_m, ln_g, ln_b, sc_f, sh_f)


def _ffn_kernel(h_ref, wg_ref, wu_ref, wo_ref, x_ref, gf_ref, g2_ref, b2_ref, sc_ref, sh_ref,
                xo_ref, ho_ref, acc_ref):
    j = pl.program_id(1)
    h = h_ref[...]
    gate = _dot(h, wg_ref[...])
    up = _dot(h, wu_ref[...])
    part = _dot((_silu(gate) * up).astype(BF16), wo_ref[...])

    @pl.when(j == 0)
    def _():
        acc_ref[...] = part

    @pl.when(j > 0)
    def _():
        acc_ref[...] += part

    @pl.when(j == pl.num_programs(1) - 1)
    def _():
        xn = _ln(DEEPNORM_ALPHA * x_ref[...] + gf_ref[...] * acc_ref[...]) * g2_ref[...] + b2_ref[...]
        xo_ref[...] = xn
        ho_ref[...] = (_ln(xn) * (1.0 + sc_ref[...]) + sh_ref[...]).astype(BF16)


def _ffn_call(h, w_in, w_out, x, g_f, ln_g, ln_b, sc_n, sh_n):
    l, d = x.shape
    f = w_out.shape[0]
    tm, th = 512, 512
    nj = f // th
    vec = pl.BlockSpec((1, d), lambda i, j: (0, 0))
    return pl.pallas_call(
        _ffn_kernel,
        grid=(l // tm, nj),
        in_specs=[pl.BlockSpec((tm, d), lambda i, j: (i, 0)),
                  pl.BlockSpec((d, th), lambda i, j: (0, j)),
                  pl.BlockSpec((d, th), lambda i, j: (0, nj + j)),
                  pl.BlockSpec((th, d), lambda i, j: (j, 0)),
                  pl.BlockSpec((tm, d), lambda i, j: (i, 0)),
                  vec, vec, vec, vec, vec],
        out_specs=[pl.BlockSpec((tm, d), lambda i, j: (i, 0)),
                   pl.BlockSpec((tm, d), lambda i, j: (i, 0))],
        out_shape=[jax.ShapeDtypeStruct((l, d), F32), jax.ShapeDtypeStruct((l, d), BF16)],
        scratch_shapes=[pltpu.VMEM((tm, d), F32)],
        compiler_params=_params(("arbitrary", "arbitrary")),
        name="ffn_ln2",
    )(h, w_in, w_in, w_out, x, g_f, ln_g, ln_b, sc_n, sh_n)


def _split_w_in(w):
    o_u = 0
    o_q = o_u + SSM_WIDTH
    o_k = o_q + GDN_KDIM
    o_v = o_k + GDN_KDIM
    o_z = o_v + GDN_VDIM
    o_beta = o_z + GDN_VDIM
    o_a = o_beta + GDN_HEADS
    o_gs = o_a + GDN_HEADS
    o_gg = o_gs + D_MODEL
    main = jnp.concatenate([w[:, o_gs:o_gg], w[:, o_gg:o_gg + D_MODEL], w[:, o_u:o_beta]], axis=1).astype(BF16)
    small = w[:, o_beta:o_gs].astype(BF16)
    small_pad = jnp.pad(small, ((0, 0), (0, LANES - SMALL_ROWS)))
    return main, small_pad, small.T


def kernel(x, c, w_ada, b_ada, w_in, ssm_lam_re, ssm_lam_im, ssm_log_dt, ssm_b_re, ssm_b_im, ssm_c_re, ssm_c_im, ssm_d, ssm_w_glu, ssm_b_glu, gdn_conv_w, gdn_a_log, gdn_dt_bias, gdn_norm_w, w_up_ssm, w_up_gdn, w_mix_out, ln1_g, ln1_b, ffn_w_in, ffn_w_out, ln2_g, ln2_b):
    bsz, seq, d = x.shape
    assert bsz == 1 and d == D_MODEL and seq % 1024 == 0
    xs = x.reshape(seq, d).astype(F32)
    mod = _ada_call(c.astype(F32), w_ada.astype(F32), b_ada.astype(F32))
    sh_m, sc_m, g_m, sh_f, sc_f, g_f = [mod[:, :, i * d:(i + 1) * d] for i in range(6)]

    h = _lnmod_call(xs, sc_m[0], sh_m[0])
    for l in range(DEPTH):
        w_main, w_small, w_small_t = _split_w_in(w_in[l])
        p_main, p_small, p_small_t = _proj_call(h, w_main, w_small, w_small_t)

        bmat, cmat, a_tab, aseg_tab = _s5_tables(ssm_lam_re[l], ssm_lam_im[l], ssm_log_dt[l], ssm_b_re[l],
                                                 ssm_b_im[l], ssm_c_re[l], ssm_c_im[l])
        ys = _s5_call(p_main, bmat, cmat, a_tab, aseg_tab, ssm_d[l].astype(F32).reshape(1, -1),
                      ssm_w_glu[l].astype(BF16), ssm_b_glu[l].astype(F32).reshape(1, -1))

        neg_a = -jnp.exp(gdn_a_log[l].astype(F32))
        dtb = gdn_dt_bias[l].astype(F32)
        zeros = jnp.zeros((GDN_HEADS,), F32)
        decay_a = jnp.concatenate([zeros, neg_a])
        decay_b = jnp.concatenate([zeros, dtb])
        rowp = jnp.pad(jnp.stack([decay_a, decay_b]), ((0, 0), (0, LANES - SMALL_ROWS)))
        colp = jnp.stack([decay_a, decay_b], axis=1)
        yg = _gdn_call(p_main, p_small, p_small_t, gdn_conv_w[l].astype(F32), rowp, colp,
                       gdn_norm_w[l].astype(F32).reshape(1, -1))

        xs, h2 = _merge_call(ys, yg, p_main, xs, w_up_ssm[l].astype(BF16), w_up_gdn[l].astype(BF16),
                             w_mix_out[l].astype(BF16), g_m[l], ln1_g[l].astype(F32).reshape(1, -1),
                             ln1_b[l].astype(F32).reshape(1, -1), sc_f[l], sh_f[l])
        nxt = min(l + 1, DEPTH - 1)
        xs, h = _ffn_call(h2, ffn_w_in[l].astype(BF16), ffn_w_out[l].astype(BF16), xs, g_f[l],
                          ln2_g[l].astype(F32).reshape(1, -1), ln2_b[l].astype(F32).reshape(1, -1),
                          sc_m[nxt], sh_m[nxt])
    return xs.reshape(bsz, seq, d)
```

```python
import functools
import math

import jax
import jax.numpy as jnp
from jax import lax
from jax.experimental import pallas as pl
from jax.experimental.pallas import tpu as pltpu

F32 = jnp.float32
BF16 = jnp.bfloat16

D_MODEL = 2048
DEPTH = 2
CHUNK = 64
SSM_WIDTH = D_MODEL // 2
SSM_GROUP = 16
SSM_GROUPS = SSM_WIDTH // SSM_GROUP
SSM_STATE = 64
GDN_HEADS = 8
GDN_DK = 128
GDN_DV = 128
GDN_KDIM = GDN_HEADS * GDN_DK
GDN_VDIM = GDN_HEADS * GDN_DV
CONV_WIDTH = 4
FFN_HIDDEN = -(-8 * D_MODEL // (3 * 256)) * 256
DEEPNORM_ALPHA = (2 * DEPTH) ** 0.25
LN_EPS = 1e-5
NORM_EPS = 1e-6

SUBLANES = 8
LANES = 128
VMEM_LIMIT_BYTES = 56 * 1024 * 1024

COL_GATE_S = 0
COL_GATE_G = D_MODEL
COL_U = 2 * D_MODEL
COL_Q = COL_U + SSM_WIDTH
COL_K = COL_Q + GDN_KDIM
COL_V = COL_K + GDN_KDIM
COL_Z = COL_V + GDN_VDIM
PROJ_MAIN = COL_Z + GDN_VDIM
SMALL_ROWS = 2 * GDN_HEADS

S5_SEG = 64
S5_BLOCK = SUBLANES * S5_SEG
S5_CLUSTERS = SSM_WIDTH // LANES
S5_CSTATE = (LANES // SSM_GROUP) * SSM_STATE

GDN_BLOCK = 128


def _params(sem):
    return pltpu.CompilerParams(dimension_semantics=sem, vmem_limit_bytes=VMEM_LIMIT_BYTES)


def _ln(x):
    mu = jnp.mean(x, axis=-1, keepdims=True)
    xc = x - mu
    var = jnp.mean(xc * xc, axis=-1, keepdims=True)
    return xc * lax.rsqrt(var + LN_EPS)


def _sigmoid(x):
    return 1.0 / (1.0 + jnp.exp(-x))


def _silu(x):
    return x * _sigmoid(x)


def _softplus(x):
    return jnp.maximum(x, 0.0) + jnp.log(1.0 + jnp.exp(-jnp.abs(x)))


def _dot(a, b):
    return jnp.dot(a, b, preferred_element_type=F32)


def _dot_nt(a, b):
    return lax.dot_general(a, b, (((1,), (1,)), ((), ())), preferred_element_type=F32)


def _ada_kernel(c_ref, w_ref, b_ref, o_ref):
    c = c_ref[...]
    s = _silu(c).astype(BF16)
    o_ref[0] = _dot(s, w_ref[0].astype(BF16)) + b_ref[0]


def _ada_call(c, w_ada, b_ada):
    nl, d, n = w_ada.shape
    tn = 1024
    c8 = jnp.broadcast_to(c, (SUBLANES, d))
    out = pl.pallas_call(
        _ada_kernel,
        grid=(nl, n // tn),
        in_specs=[pl.BlockSpec((SUBLANES, d), lambda l, j: (0, 0)),
                  pl.BlockSpec((1, d, tn), lambda l, j: (l, 0, j)),
                  pl.BlockSpec((1, 1, tn), lambda l, j: (l, 0, j))],
        out_specs=pl.BlockSpec((1, SUBLANES, tn), lambda l, j: (l, 0, j)),
        out_shape=jax.ShapeDtypeStruct((nl, SUBLANES, n), F32),
        compiler_params=_params(("arbitrary", "arbitrary")),
        name="adaln_mod",
    )(c8, w_ada, b_ada.reshape(nl, 1, n))
    return out[:, 0:1, :]


def _lnmod_kernel(x_ref, sc_ref, sh_ref, h_ref):
    y = _ln(x_ref[...])
    h_ref[...] = (y * (1.0 + sc_ref[...]) + sh_ref[...]).astype(BF16)


def _lnmod_call(x, sc, sh):
    l, d = x.shape
    tm = 512
    return pl.pallas_call(
        _lnmod_kernel,
        grid=(l // tm,),
        in_specs=[pl.BlockSpec((tm, d), lambda i: (i, 0)),
                  pl.BlockSpec((1, d), lambda i: (0, 0)),
                  pl.BlockSpec((1, d), lambda i: (0, 0))],
        out_specs=pl.BlockSpec((tm, d), lambda i: (i, 0)),
        out_shape=jax.ShapeDtypeStruct((l, d), BF16),
        compiler_params=_params(("arbitrary",)),
        name="ln0_mod",
    )(x, sc, sh)


def _proj_kernel(h_ref, wm_ref, ws_ref, wst_ref, pm_ref, ps_ref, pst_ref):
    h = h_ref[...]
    pm_ref[...] = _dot(h, wm_ref[...])

    @pl.when(pl.program_id(1) == 0)
    def _():
        ps_ref[...] = _dot(h, ws_ref[...])
        pst_ref[...] = _dot_nt(wst_ref[...], h)


def _proj_call(h, w_main, w_small, w_small_t):
    l, d = h.shape
    n = w_main.shape[1]
    tm, tn = 1024, 1024
    return pl.pallas_call(
        _proj_kernel,
        grid=(l // tm, n // tn),
        in_specs=[pl.BlockSpec((tm, d), lambda i, j: (i, 0)),
                  pl.BlockSpec((d, tn), lambda i, j: (0, j)),
                  pl.BlockSpec((d, LANES), lambda i, j: (0, 0)),
                  pl.BlockSpec((SMALL_ROWS, d), lambda i, j: (0, 0))],
        out_specs=[pl.BlockSpec((tm, tn), lambda i, j: (i, j)),
                   pl.BlockSpec((tm, LANES), lambda i, j: (i, 0)),
                   pl.BlockSpec((SMALL_ROWS, tm), lambda i, j: (0, i))],
        out_shape=[jax.ShapeDtypeStruct((l, n), F32),
                   jax.ShapeDtypeStruct((l, LANES), F32),
                   jax.ShapeDtypeStruct((SMALL_ROWS, l), F32)],
        compiler_params=_params(("arbitrary", "arbitrary")),
        name="in_proj",
    )(h, w_main, w_small, w_small_t)


def _s5_kernel(*refs):
    u_refs = refs[:S5_CLUSTERS]
    (bmat_ref, cmat_ref, a_ref, aseg_ref, dskip_ref, wglu_ref, bglu_ref, o_ref,
     bu_ref, z_ref, carry_ref, outp_ref) = refs[S5_CLUSTERS:]

    @pl.when(pl.program_id(0) == 0)
    def _():
        carry_ref[...] = jnp.zeros_like(carry_ref)

    sub = lax.broadcasted_iota(jnp.int32, (SUBLANES, S5_CSTATE), 0)
    half = S5_CSTATE

    for cg in range(S5_CLUSTERS):
        cols = slice(cg * LANES, (cg + 1) * LANES)
        ucl = jnp.concatenate(
            [u_refs[cg][pl.ds(i, SUBLANES, stride=S5_SEG), :] for i in range(S5_SEG)], axis=0)
        bu_ref[...] = _dot(ucl.astype(BF16), bmat_ref[cg])
        ar = jnp.broadcast_to(a_ref[cg, 0:1, :], (SUBLANES, half))
        ai = jnp.broadcast_to(a_ref[cg, 1:2, :], (SUBLANES, half))

        def load_bu(i):
            r0 = pl.multiple_of(i * SUBLANES, SUBLANES)
            return bu_ref[pl.ds(r0, SUBLANES), 0:half], bu_ref[pl.ds(r0, SUBLANES), half:2 * half]

        def end_state_step(i, carry):
            xr, xi = carry
            bur, bui = load_bu(i)
            return ar * xr - ai * xi + bur, ar * xi + ai * xr + bui

        zero = jnp.zeros((SUBLANES, half), F32)
        er, ei = lax.fori_loop(0, S5_SEG, end_state_step, (zero, zero), unroll=8)

        pr = jnp.broadcast_to(aseg_ref[cg, 0:1, :], (SUBLANES, half))
        pi = jnp.broadcast_to(aseg_ref[cg, 1:2, :], (SUBLANES, half))
        cr = carry_ref[cg, :, 0:half]
        ci = carry_ref[cg, :, half:2 * half]
        sr, si = zero, zero
        for s in range(SUBLANES):
            sr = jnp.where(sub == s, cr, sr)
            si = jnp.where(sub == s, ci, si)
            esr = jnp.broadcast_to(er[s:s + 1, :], (SUBLANES, half))
            esi = jnp.broadcast_to(ei[s:s + 1, :], (SUBLANES, half))
            cr, ci = pr * cr - pi * ci + esr, pr * ci + pi * cr + esi
        carry_ref[cg, :, 0:half] = cr
        carry_ref[cg, :, half:2 * half] = ci

        def state_step(i, carry):
            xr, xi = carry
            bur, bui = load_bu(i)
            nxr = ar * xr - ai * xi + bur
            nxi = ar * xi + ai * xr + bui
            r0 = pl.multiple_of(i * SUBLANES, SUBLANES)
            bu_ref[pl.ds(r0, SUBLANES), 0:half] = nxr
            bu_ref[pl.ds(r0, SUBLANES), half:2 * half] = nxi
            return nxr, nxi

        lax.fori_loop(0, S5_SEG, state_step, (sr, si), unroll=8)

        y = _dot(bu_ref[...].astype(BF16), cmat_ref[cg]) + dskip_ref[:, cols] * ucl
        z_ref[:, cols] = jax.nn.gelu(y, approximate=True)

    z = z_ref[...]
    gate = _dot(z.astype(BF16), wglu_ref[...]) + bglu_ref[...]
    outp = z * _sigmoid(gate)
    for cg in range(S5_CLUSTERS):
        cols = slice(cg * LANES, (cg + 1) * LANES)
        for i in range(S5_SEG):
            outp_ref[cg, pl.ds(i, SUBLANES, stride=S5_SEG), :] = outp[i * SUBLANES:(i + 1) * SUBLANES, cols]
        o_ref[:, cols] = outp_ref[cg].astype(BF16)


def _s5_call(p_main, bmat, cmat, a_tab, aseg_tab, d_skip, w_glu, b_glu):
    l = p_main.shape[0]
    w = SSM_WIDTH
    full = lambda *shape: pl.BlockSpec(shape, lambda b: (0,) * len(shape))
    return pl.pallas_call(
        _s5_kernel,
        grid=(l // S5_BLOCK,),
        in_specs=[pl.BlockSpec((S5_BLOCK, LANES), functools.partial(lambda cg, b: (b, COL_U // LANES + cg), cg))
                  for cg in range(S5_CLUSTERS)] + [
                  full(S5_CLUSTERS, LANES, 2 * S5_CSTATE),
                  full(S5_CLUSTERS, 2 * S5_CSTATE, LANES),
                  full(S5_CLUSTERS, 2, S5_CSTATE),
                  full(S5_CLUSTERS, 2, S5_CSTATE),
                  full(1, w), full(w, w), full(1, w)],
        out_specs=pl.BlockSpec((S5_BLOCK, w), lambda b: (b, 0)),
        out_shape=jax.ShapeDtypeStruct((l, w), BF16),
        scratch_shapes=[pltpu.VMEM((S5_BLOCK, 2 * S5_CSTATE), F32),
                        pltpu.VMEM((S5_BLOCK, w), F32),
                        pltpu.VMEM((S5_CLUSTERS, SUBLANES, 2 * S5_CSTATE), F32),
                        pltpu.VMEM((S5_CLUSTERS, S5_BLOCK, LANES), F32)],
        compiler_params=_params(("arbitrary",)),
        name="s5_mixer",
    )(*([p_main] * S5_CLUSTERS), bmat, cmat, a_tab, aseg_tab, d_skip, w_glu, b_glu)


def _s5_tables(lam_re, lam_im, log_dt, b_re, b_im, c_re, c_im):
    lr, li = lam_re.astype(F32), lam_im.astype(F32)
    dt = jnp.exp(log_dt.astype(F32))[:, None]
    mag = jnp.exp(lr * dt)
    ang = li * dt
    ab_re, ab_im = mag * jnp.cos(ang), mag * jnp.sin(ang)
    den = lr * lr + li * li
    nr, ni = ab_re - 1.0, ab_im
    f_re = (nr * lr + ni * li) / den
    f_im = (ni * lr - nr * li) / den
    br, bi = b_re.astype(F32), b_im.astype(F32)
    bb_re = f_re[..., None] * br - f_im[..., None] * bi
    bb_im = f_re[..., None] * bi + f_im[..., None] * br
    gpc = LANES // SSM_GROUP
    eye = jnp.eye(gpc, dtype=F32)

    def in_block(bb):
        t = bb.reshape(S5_CLUSTERS, gpc, SSM_STATE, SSM_GROUP)
        m = jnp.einsum('cgph,gk->cghkp', t, eye)
        return m.reshape(S5_CLUSTERS, gpc * SSM_GROUP, gpc * SSM_STATE)

    def out_block(cc):
        t = cc.reshape(S5_CLUSTERS, gpc, SSM_GROUP, SSM_STATE)
        m = jnp.einsum('cghp,gk->cgpkh', t, eye)
        return m.reshape(S5_CLUSTERS, gpc * SSM_STATE, gpc * SSM_GROUP)

    bmat = jnp.concatenate([in_block(bb_re), in_block(bb_im)], axis=-1).astype(BF16)
    cmat = jnp.concatenate([out_block(c_re.astype(F32)), -out_block(c_im.astype(F32))], axis=1).astype(BF16)

    def tab(re, im):
        return jnp.stack([re.reshape(S5_CLUSTERS, S5_CSTATE), im.reshape(S5_CLUSTERS, S5_CSTATE)], axis=1)

    a_tab = tab(ab_re, ab_im)
    pr, pi = ab_re, ab_im
    for _ in range(int(math.log2(S5_SEG))):
        pr, pi = pr * pr - pi * pi, 2.0 * pr * pi
    return bmat, cmat, a_tab, tab(pr, pi)


def _tri_inverse_batch(ms, row, col):
    blk = 16

    def block_id(idx, size):
        return jnp.right_shift(idx, int(math.log2(size)))

    same = block_id(row, blk) == block_id(col, blk)
    eye = jnp.where(row == col, 1.0, 0.0)
    ps = [jnp.where(same, -m, 0.0) for m in ms]
    xs = [eye + p for p in ps]
    for _ in range(3):
        pbs = [p.astype(BF16) for p in ps]
        ps = [_dot(pb, pb) for pb in pbs]
        xs = [x + _dot(p.astype(BF16), x.astype(BF16)) for p, x in zip(ps, xs)]
    while blk < CHUNK:
        join = jnp.where(block_id(row, 2 * blk) == block_id(col, 2 * blk),
                         jnp.where(block_id(row, blk) != block_id(col, blk), 1.0, 0.0), 0.0)
        cbs = [(m * join).astype(BF16) for m in ms]
        xbs = [x.astype(BF16) for x in xs]
        ts = [_dot(xb, cb).astype(BF16) for xb, cb in zip(xbs, cbs)]
        xs = [x - _dot(t, xb) for x, t, xb in zip(xs, ts, xbs)]
        blk *= 2
    return xs


def _gdn_kernel(q_ref, k_ref, v_ref, z_ref, ps_ref, pst_ref, cw_ref, rowp_ref, colp_ref, nw_ref, o_ref,
                s_ref, tq_ref, tk_ref, tv_ref):
    r = GDN_BLOCK
    nchunks = r // CHUNK
    heads = range(GDN_HEADS)

    @pl.when(pl.program_id(0) == 0)
    def _():
        s_ref[...] = jnp.zeros_like(s_ref)
        tq_ref[...] = jnp.zeros_like(tq_ref)
        tk_ref[...] = jnp.zeros_like(tk_ref)
        tv_ref[...] = jnp.zeros_like(tv_ref)

    def conv_silu(x_ref, tail_ref, col0):
        x = x_ref[...]
        ext = jnp.concatenate([tail_ref[...], x], axis=0)
        w = cw_ref[:, col0:col0 + GDN_KDIM]
        y = w[CONV_WIDTH - 1:CONV_WIDTH, :] * x
        for j in range(1, CONV_WIDTH):
            y = y + w[CONV_WIDTH - 1 - j:CONV_WIDTH - j, :] * ext[SUBLANES - j:SUBLANES - j + r, :]
        tail_ref[...] = x[r - SUBLANES:r, :]
        return _silu(y)

    q = conv_silu(q_ref, tq_ref, 0)
    k = conv_silu(k_ref, tk_ref, GDN_KDIM)
    v = conv_silu(v_ref, tv_ref, 2 * GDN_KDIM)

    ps = ps_ref[...]
    rowi = lax.broadcasted_iota(jnp.int32, (r, LANES), 0)
    g_all = rowp_ref[0:1, :] * _softplus(ps + rowp_ref[1:2, :])
    for sh in (1, 2, 4, 8, 16, 32):
        g_all = g_all + jnp.where((rowi & (CHUNK - 1)) >= sh, pltpu.roll(g_all, sh, 0), 0.0)
    beta_all = _sigmoid(ps)
    pst = pst_ref[...]
    lane_t = lax.broadcasted_iota(jnp.int32, (SMALL_ROWS, r), 1)
    g_t = colp_ref[:, 0:1] * _softplus(pst + colp_ref[:, 1:2])
    for sh in (1, 2, 4, 8, 16, 32):
        g_t = g_t + jnp.where((lane_t & (CHUNK - 1)) >= sh, pltpu.roll(g_t, sh, 1), 0.0)

    row = lax.broadcasted_iota(jnp.int32, (CHUNK, CHUNK), 0)
    col = lax.broadcasted_iota(jnp.int32, (CHUNK, CHUNK), 1)
    causal = row >= col
    strict = row > col
    nw = nw_ref[...]

    qn, kn, vh = [], [], []
    for h in heads:
        hl = slice(h * GDN_DK, (h + 1) * GDN_DK)
        qh, kh = q[:, hl], k[:, hl]
        qn.append(qh * lax.rsqrt(jnp.sum(qh * qh, axis=-1, keepdims=True) + NORM_EPS) * (GDN_DK ** -0.5))
        kn.append(kh * lax.rsqrt(jnp.sum(kh * kh, axis=-1, keepdims=True) + NORM_EPS))
        vh.append(v[:, hl])

    items = [(c, h) for c in range(nchunks) for h in heads]
    it = {}
    for (c, h) in items:
        rows = slice(c * CHUNK, (c + 1) * CHUNK)
        d = {}
        d['q'], d['k'], d['v'] = qn[h][rows], kn[h][rows], vh[h][rows]
        d['beta'] = beta_all[rows, h:h + 1]
        gcc = g_all[rows, GDN_HEADS + h:GDN_HEADS + h + 1]
        gcr = g_t[GDN_HEADS + h:GDN_HEADS + h + 1, c * CHUNK:(c + 1) * CHUNK]
        d['g_end'] = gcc[CHUNK - 1:CHUNK, :]
        d['eg'] = jnp.exp(gcc)
        d['etail'] = jnp.exp(d['g_end'] - gcc)
        d['decay'] = jnp.where(causal, jnp.exp(jnp.where(causal, gcc - gcr, 0.0)), 0.0)
        d['kb'] = d['k'] * d['beta']
        it[(c, h)] = d

    kqs = [_dot_nt(jnp.concatenate([it[i]['kb'], it[i]['q']], axis=0).astype(BF16), it[i]['k'].astype(BF16))
           for i in items]
    ms = [jnp.where(strict, kq[:CHUNK] * it[i]['decay'], 0.0) for kq, i in zip(kqs, items)]
    attns = [(kq[CHUNK:] * it[i]['decay']).astype(BF16) for kq, i in zip(kqs, items)]
    t_invs = _tri_inverse_batch(ms, row, col)
    uws = [_dot(t.astype(BF16),
                jnp.concatenate([it[i]['v'] * it[i]['beta'], it[i]['kb'] * it[i]['eg']], axis=1).astype(BF16))
           for t, i in zip(t_invs, items)]
    for i, uw, attn in zip(items, uws, attns):
        d = it[i]
        d['u'] = uw[:, :GDN_DV]
        d['wqd'] = jnp.concatenate([uw[:, GDN_DV:], d['q'] * d['eg']], axis=0).astype(BF16)
        d['kt'] = (d['k'] * d['etail']).astype(BF16)
        d['attn'] = attn

    states = [s_ref[h] for h in heads]
    for c in range(nchunks):
        rows = slice(c * CHUNK, (c + 1) * CHUNK)
        wqs = [_dot(it[(c, h)]['wqd'], states[h].astype(BF16)) for h in heads]
        v_news = [it[(c, h)]['u'] - wqs[h][:CHUNK] for h in heads]
        v_bfs = [vn.astype(BF16) for vn in v_news]
        o_cs = [wqs[h][CHUNK:] + _dot(it[(c, h)]['attn'], v_bfs[h]) for h in heads]
        states = [states[h] * jnp.exp(it[(c, h)]['g_end']) + lax.dot_general(
            it[(c, h)]['kt'], v_bfs[h], (((0,), (0,)), ((), ())), preferred_element_type=F32) for h in heads]
        for h in heads:
            hl = slice(h * GDN_DV, (h + 1) * GDN_DV)
            o_c = o_cs[h]
            o_n = o_c * lax.rsqrt(jnp.mean(o_c * o_c, axis=-1, keepdims=True) + NORM_EPS) * nw
            o_ref[rows, hl] = (o_n * _silu(z_ref[rows, hl])).astype(BF16)
    for h in heads:
        s_ref[h] = states[h]


def _gdn_call(p_main, p_small, p_small_t, conv_w, rowp, colp, norm_w):
    l = p_main.shape[0]
    r = GDN_BLOCK
    wide = lambda base: pl.BlockSpec((r, GDN_KDIM), lambda b: (b, base // GDN_KDIM))
    full = lambda *shape: pl.BlockSpec(shape, lambda b: (0,) * len(shape))
    return pl.pallas_call(
        _gdn_kernel,
        grid=(l // r,),
        in_specs=[wide(COL_Q), wide(COL_K), wide(COL_V), wide(COL_Z),
                  pl.BlockSpec((r, LANES), lambda b: (b, 0)),
                  pl.BlockSpec((SMALL_ROWS, r), lambda b: (0, b)),
                  full(CONV_WIDTH, 2 * GDN_KDIM + GDN_VDIM),
                  full(2, LANES), full(SMALL_ROWS, 2), full(1, GDN_DV)],
        out_specs=pl.BlockSpec((r, GDN_VDIM), lambda b: (b, 0)),
        out_shape=jax.ShapeDtypeStruct((l, GDN_VDIM), BF16),
        scratch_shapes=[pltpu.VMEM((GDN_HEADS, GDN_DK, GDN_DV), F32),
                        pltpu.VMEM((SUBLANES, GDN_KDIM), F32),
                        pltpu.VMEM((SUBLANES, GDN_KDIM), F32),
                        pltpu.VMEM((SUBLANES, GDN_VDIM), F32)],
        compiler_params=_params(("arbitrary",)),
        name="gated_deltanet",
    )(p_main, p_main, p_main, p_main, p_small, p_small_t, conv_w, rowp, colp, norm_w)


def _merge_kernel(ys_ref, yg_ref, gs_ref, gg_ref, x_ref, wus_ref, wug_ref, wo_ref,
                  gm_ref, g1_ref, b1_ref, sc_ref, sh_ref, xo_ref, ho_ref):
    a = _dot(ys_ref[...], wus_ref[...])
    b = _dot(yg_ref[...], wug_ref[...])
    m = _sigmoid(gs_ref[...]) * a + _sigmoid(gg_ref[...]) * b
    y = _dot(m.astype(BF16), wo_ref[...])
    xn = _ln(DEEPNORM_ALPHA * x_ref[...] + gm_ref[...] * y) * g1_ref[...] + b1_ref[...]
    xo_ref[...] = xn
    ho_ref[...] = (_ln(xn) * (1.0 + sc_ref[...]) + sh_ref[...]).astype(BF16)


def _merge_call(ys, yg, p_main, x, w_us, w_ug, w_out, g_m, ln_g, ln_b, sc_f, sh_f):
    l, d = x.shape
    tm = 256
    vec = pl.BlockSpec((1, d), lambda i: (0, 0))
    once = lambda shape: pl.BlockSpec(shape, lambda i: (0, 0), pipeline_mode=pl.Buffered(1))
    return pl.pallas_call(
        _merge_kernel,
        grid=(l // tm,),
        in_specs=[pl.BlockSpec((tm, SSM_WIDTH), lambda i: (i, 0)),
                  pl.BlockSpec((tm, GDN_VDIM), lambda i: (i, 0)),
                  pl.BlockSpec((tm, d), lambda i: (i, COL_GATE_S // d)),
                  pl.BlockSpec((tm, d), lambda i: (i, COL_GATE_G // d)),
                  pl.BlockSpec((tm, d), lambda i: (i, 0)),
                  once((SSM_WIDTH, d)), once((GDN_VDIM, d)), once((d, d)),
                  vec, vec, vec, vec, vec],
        out_specs=[pl.BlockSpec((tm, d), lambda i: (i, 0)),
                   pl.BlockSpec((tm, d), lambda i: (i, 0))],
        out_shape=[jax.ShapeDtypeStruct((l, d), F32), jax.ShapeDtypeStruct((l, d), BF16)],
        compiler_params=_params(("arbitrary",)),
        name="merge_out_ln1",
    )(ys, yg, p_main, p_main, x, w_us, w_ug, w_out, g_m, ln_g, ln_b, sc_f, sh_f)


def _ffn_kernel(h_ref, wg_ref, wu_ref, wo_ref, x_ref, gf_ref, g2_ref, b2_ref, sc_ref, sh_ref,
                xo_ref, ho_ref, acc_ref):
    j = pl.program_id(1)
    h = h_ref[...]
    gate = _dot(h, wg_ref[...])
    up = _dot(h, wu_ref[...])
    part = _dot((_silu(gate) * up).astype(BF16), wo_ref[...])

    @pl.when(j == 0)
    def _():
        acc_ref[...] = part

    @pl.when(j > 0)
    def _():
        acc_ref[...] += part

    @pl.when(j == pl.num_programs(1) - 1)
    def _():
        xn = _ln(DEEPNORM_ALPHA * x_ref[...] + gf_ref[...] * acc_ref[...]) * g2_ref[...] + b2_ref[...]
        xo_ref[...] = xn
        ho_ref[...] = (_ln(xn) * (1.0 + sc_ref[...]) + sh_ref[...]).astype(BF16)


def _ffn_call(h, w_in, w_out, x, g_f, ln_g, ln_b, sc_n, sh_n):
    l, d = x.shape
    f = w_out.shape[0]
    tm, th = 512, 512
    nj = f // th
    vec = pl.BlockSpec((1, d), lambda i, j: (0, 0))
    return pl.pallas_call(
        _ffn_kernel,
        grid=(l // tm, nj),
        in_specs=[pl.BlockSpec((tm, d), lambda i, j: (i, 0)),
                  pl.BlockSpec((d, th), lambda i, j: (0, j)),
                  pl.BlockSpec((d, th), lambda i, j: (0, nj + j)),
                  pl.BlockSpec((th, d), lambda i, j: (j, 0)),
                  pl.BlockSpec((tm, d), lambda i, j: (i, 0)),
                  vec, vec, vec, vec, vec],
        out_specs=[pl.BlockSpec((tm, d), lambda i, j: (i, 0)),
                   pl.BlockSpec((tm, d), lambda i, j: (i, 0))],
        out_shape=[jax.ShapeDtypeStruct((l, d), F32), jax.ShapeDtypeStruct((l, d), BF16)],
        scratch_shapes=[pltpu.VMEM((tm, d), F32)],
        compiler_params=_params(("arbitrary", "arbitrary")),
        name="ffn_ln2",
    )(h, w_in, w_in, w_out, x, g_f, ln_g, ln_b, sc_n, sh_n)


def _split_w_in(w):
    o_u = 0
    o_q = o_u + SSM_WIDTH
    o_k = o_q + GDN_KDIM
    o_v = o_k + GDN_KDIM
    o_z = o_v + GDN_VDIM
    o_beta = o_z + GDN_VDIM
    o_a = o_beta + GDN_HEADS
    o_gs = o_a + GDN_HEADS
    o_gg = o_gs + D_MODEL
    main = jnp.concatenate([w[:, o_gs:o_gg], w[:, o_gg:o_gg + D_MODEL], w[:, o_u:o_beta]], axis=1).astype(BF16)
    small = w[:, o_beta:o_gs].astype(BF16)
    small_pad = jnp.pad(small, ((0, 0), (0, LANES - SMALL_ROWS)))
    return main, small_pad, small.T


def kernel(x, c, w_ada, b_ada, w_in, ssm_lam_re, ssm_lam_im, ssm_log_dt, ssm_b_re, ssm_b_im, ssm_c_re, ssm_c_im, ssm_d, ssm_w_glu, ssm_b_glu, gdn_conv_w, gdn_a_log, gdn_dt_bias, gdn_norm_w, w_up_ssm, w_up_gdn, w_mix_out, ln1_g, ln1_b, ffn_w_in, ffn_w_out, ln2_g, ln2_b):
    bsz, seq, d = x.shape
    assert bsz == 1 and d == D_MODEL and seq % 1024 == 0
    xs = x.reshape(seq, d).astype(F32)
    mod = _ada_call(c.astype(F32), w_ada.astype(F32), b_ada.astype(F32))
    sh_m, sc_m, g_m, sh_f, sc_f, g_f = [mod[:, :, i * d:(i + 1) * d] for i in range(6)]

    h = _lnmod_call(xs, sc_m[0], sh_m[0])
    for l in range(DEPTH):
        w_main, w_small, w_small_t = _split_w_in(w_in[l])
        p_main, p_small, p_small_t = _proj_call(h, w_main, w_small, w_small_t)

        bmat, cmat, a_tab, aseg_tab = _s5_tables(ssm_lam_re[l], ssm_lam_im[l], ssm_log_dt[l], ssm_b_re[l],
                                                 ssm_b_im[l], ssm_c_re[l], ssm_c_im[l])
        ys = _s5_call(p_main, bmat, cmat, a_tab, aseg_tab, ssm_d[l].astype(F32).reshape(1, -1),
                      ssm_w_glu[l].astype(BF16), ssm_b_glu[l].astype(F32).reshape(1, -1))

        neg_a = -jnp.exp(gdn_a_log[l].astype(F32))
        dtb = gdn_dt_bias[l].astype(F32)
        zeros = jnp.zeros((GDN_HEADS,), F32)
        decay_a = jnp.concatenate([zeros, neg_a])
        decay_b = jnp.concatenate([zeros, dtb])
        rowp = jnp.pad(jnp.stack([decay_a, decay_b]), ((0, 0), (0, LANES - SMALL_ROWS)))
        colp = jnp.stack([decay_a, decay_b], axis=1)
        yg = _gdn_call(p_main, p_small, p_small_t, gdn_conv_w[l].astype(F32), rowp, colp,
                       gdn_norm_w[l].astype(F32).reshape(1, -1))

        xs, h2 = _merge_call(ys, yg, p_main, xs, w_up_ssm[l].astype(BF16), w_up_gdn[l].astype(BF16),
                             w_mix_out[l].astype(BF16), g_m[l], ln1_g[l].astype(F32).reshape(1, -1),
                             ln1_b[l].astype(F32).reshape(1, -1), sc_f[l], sh_f[l])
        nxt = min(l + 1, DEPTH - 1)
        xs, h = _ffn_call(h2, ffn_w_in[l].astype(BF16), ffn_w_out[l].astype(BF16), xs, g_f[l],
                          ln2_g[l].astype(F32).reshape(1, -1), ln2_b[l].astype(F32).reshape(1, -1),
                          sc_m[nxt], sh_m[nxt])
    return xs.reshape(bsz, seq, d)
```

```python
import functools
import math

import jax
import jax.numpy as jnp
from jax import lax
from jax.experimental import pallas as pl
from jax.experimental.pallas import tpu as pltpu

F32 = jnp.float32
BF16 = jnp.bfloat16

D_MODEL = 2048
DEPTH = 2
CHUNK = 64
SSM_WIDTH = D_MODEL // 2
SSM_GROUP = 16
SSM_GROUPS = SSM_WIDTH // SSM_GROUP
SSM_STATE = 64
GDN_HEADS = 8
GDN_DK = 128
GDN_DV = 128
GDN_KDIM = GDN_HEADS * GDN_DK
GDN_VDIM = GDN_HEADS * GDN_DV
CONV_WIDTH = 4
FFN_HIDDEN = -(-8 * D_MODEL // (3 * 256)) * 256
DEEPNORM_ALPHA = (2 * DEPTH) ** 0.25
LN_EPS = 1e-5
NORM_EPS = 1e-6

SUBLANES = 8
LANES = 128
VMEM_LIMIT_BYTES = 56 * 1024 * 1024

COL_GATE_S = 0
COL_GATE_G = D_MODEL
COL_U = 2 * D_MODEL
COL_Q = COL_U + SSM_WIDTH
COL_K = COL_Q + GDN_KDIM
COL_V = COL_K + GDN_KDIM
COL_Z = COL_V + GDN_VDIM
PROJ_MAIN = COL_Z + GDN_VDIM
SMALL_ROWS = 2 * GDN_HEADS

S5_SEG = 64
S5_BLOCK = SUBLANES * S5_SEG
S5_CLUSTERS = SSM_WIDTH // LANES
S5_CSTATE = (LANES // SSM_GROUP) * SSM_STATE

GDN_BLOCK = 128


def _params(sem):
    return pltpu.CompilerParams(dimension_semantics=sem, vmem_limit_bytes=VMEM_LIMIT_BYTES)


def _ln(x):
    mu = jnp.mean(x, axis=-1, keepdims=True)
    xc = x - mu
    var = jnp.mean(xc * xc, axis=-1, keepdims=True)
    return xc * lax.rsqrt(var + LN_EPS)


def _sigmoid(x):
    return 1.0 / (1.0 + jnp.exp(-x))


def _silu(x):
    return x * _sigmoid(x)


def _softplus(x):
    return jnp.maximum(x, 0.0) + jnp.log(1.0 + jnp.exp(-jnp.abs(x)))


def _dot(a, b):
    return jnp.dot(a, b, preferred_element_type=F32)


def _dot_nt(a, b):
    return lax.dot_general(a, b, (((1,), (1,)), ((), ())), preferred_element_type=F32)


def _ada_kernel(c_ref, w_ref, b_ref, o_ref):
    c = c_ref[...]
    s = _silu(c).astype(BF16)
    o_ref[0] = _dot(s, w_ref[0].astype(BF16)) + b_ref[0]


def _ada_call(c, w_ada, b_ada):
    nl, d, n = w_ada.shape
    tn = 1024
    c8 = jnp.broadcast_to(c, (SUBLANES, d))
    out = pl.pallas_call(
        _ada_kernel,
        grid=(nl, n // tn),
        in_specs=[pl.BlockSpec((SUBLANES, d), lambda l, j: (0, 0)),
                  pl.BlockSpec((1, d, tn), lambda l, j: (l, 0, j)),
                  pl.BlockSpec((1, 1, tn), lambda l, j: (l, 0, j))],
        out_specs=pl.BlockSpec((1, SUBLANES, tn), lambda l, j: (l, 0, j)),
        out_shape=jax.ShapeDtypeStruct((nl, SUBLANES, n), F32),
        compiler_params=_params(("arbitrary", "arbitrary")),
        name="adaln_mod",
    )(c8, w_ada, b_ada.reshape(nl, 1, n))
    return out[:, 0:1, :]


def _lnmod_kernel(x_ref, sc_ref, sh_ref, h_ref):
    y = _ln(x_ref[...])
    h_ref[...] = (y * (1.0 + sc_ref[...]) + sh_ref[...]).astype(BF16)


def _lnmod_call(x, sc, sh):
    l, d = x.shape
    tm = 512
    return pl.pallas_call(
        _lnmod_kernel,
        grid=(l // tm,),
        in_specs=[pl.BlockSpec((tm, d), lambda i: (i, 0)),
                  pl.BlockSpec((1, d), lambda i: (0, 0)),
                  pl.BlockSpec((1, d), lambda i: (0, 0))],
        out_specs=pl.BlockSpec((tm, d), lambda i: (i, 0)),
        out_shape=jax.ShapeDtypeStruct((l, d), BF16),
        compiler_params=_params(("arbitrary",)),
        name="ln0_mod",
    )(x, sc, sh)


PROJ_TN = 1024
PROJ_GATE_TILES = 2 * D_MODEL // PROJ_TN


def _proj_kernel(h_ref, wg_ref, wa_ref, ws_ref, pm_ref, ps_ref, pst_ref):
    j = pl.program_id(1)
    h = h_ref[...]

    @pl.when(j < PROJ_GATE_TILES)
    def _():
        pm_ref[...] = _dot(h, wg_ref[...])

    @pl.when(j >= PROJ_GATE_TILES)
    def _():
        pm_ref[...] = _dot(h, wa_ref[...])

    @pl.when(j == 0)
    def _():
        ps = _dot(h, ws_ref[...])
        ps_ref[...] = ps
        pst_ref[...] = ps.T[:SMALL_ROWS, :]


def _proj_call(h, w_gates, w_act, w_small):
    l, d = h.shape
    tm, tn = 1024, PROJ_TN
    ng = PROJ_GATE_TILES
    n = w_gates.shape[1] + w_act.shape[1]
    return pl.pallas_call(
        _proj_kernel,
        grid=(l // tm, n // tn),
        in_specs=[pl.BlockSpec((tm, d), lambda i, j: (i, 0)),
                  pl.BlockSpec((d, tn), lambda i, j: (0, jnp.minimum(j, ng - 1))),
                  pl.BlockSpec((d, tn), lambda i, j: (0, jnp.maximum(j - ng, 0))),
                  pl.BlockSpec((d, LANES), lambda i, j: (0, 0))],
        out_specs=[pl.BlockSpec((tm, tn), lambda i, j: (i, j)),
                   pl.BlockSpec((tm, LANES), lambda i, j: (i, 0)),
                   pl.BlockSpec((SMALL_ROWS, tm), lambda i, j: (0, i))],
        out_shape=[jax.ShapeDtypeStruct((l, n), F32),
                   jax.ShapeDtypeStruct((l, LANES), F32),
                   jax.ShapeDtypeStruct((SMALL_ROWS, l), F32)],
        compiler_params=_params(("arbitrary", "arbitrary")),
        name="in_proj",
    )(h, w_gates, w_act, w_small)


def _s5_kernel(*refs):
    u_refs = refs[:S5_CLUSTERS]
    (bmat_ref, cmat_ref, a_ref, aseg_ref, dskip_ref, wglu_ref, bglu_ref, o_ref,
     bu_ref, z_ref, carry_ref, outp_ref) = refs[S5_CLUSTERS:]

    @pl.when(pl.program_id(0) == 0)
    def _():
        carry_ref[...] = jnp.zeros_like(carry_ref)

    sub = lax.broadcasted_iota(jnp.int32, (SUBLANES, S5_CSTATE), 0)
    half = S5_CSTATE
    zero = jnp.zeros((SUBLANES, half), F32)

    def in_proj(cg):
        ucl = jnp.concatenate(
            [u_refs[cg][pl.ds(i, SUBLANES, stride=S5_SEG), :] for i in range(S5_SEG)], axis=0)
        bu_ref[cg % 2] = _dot(ucl.astype(BF16), bmat_ref[cg])
        return ucl

    ucls = {0: in_proj(0)}
    gate = None
    for cg in range(S5_CLUSTERS):
        cols = slice(cg * LANES, (cg + 1) * LANES)
        buf = bu_ref.at[cg % 2]
        if cg + 1 < S5_CLUSTERS:
            ucls[cg + 1] = in_proj(cg + 1)
        ar = jnp.broadcast_to(a_ref[cg, 0:1, :], (SUBLANES, half))
        ai = jnp.broadcast_to(a_ref[cg, 1:2, :], (SUBLANES, half))

        def step(i, xr, xi):
            rows = slice(i * SUBLANES, (i + 1) * SUBLANES)
            bur, bui = buf[rows, 0:half], buf[rows, half:2 * half]
            return ar * xr - ai * xi + bur, ar * xi + ai * xr + bui

        er, ei = zero, zero
        for i in range(S5_SEG):
            er, ei = step(i, er, ei)

        pr = jnp.broadcast_to(aseg_ref[cg, 0:1, :], (SUBLANES, half))
        pi = jnp.broadcast_to(aseg_ref[cg, 1:2, :], (SUBLANES, half))
        cr = carry_ref[cg, :, 0:half]
        ci = carry_ref[cg, :, half:2 * half]
        xr, xi = zero, zero
        for s in range(SUBLANES):
            xr = jnp.where(sub == s, cr, xr)
            xi = jnp.where(sub == s, ci, xi)
            esr = jnp.broadcast_to(er[s:s + 1, :], (SUBLANES, half))
            esi = jnp.broadcast_to(ei[s:s + 1, :], (SUBLANES, half))
            cr, ci = pr * cr - pi * ci + esr, pr * ci + pi * cr + esi
        carry_ref[cg, :, 0:half] = cr
        carry_ref[cg, :, half:2 * half] = ci

        for i in range(S5_SEG):
            xr, xi = step(i, xr, xi)
            rows = slice(i * SUBLANES, (i + 1) * SUBLANES)
            buf[rows, 0:half] = xr
            buf[rows, half:2 * half] = xi

        y = _dot(buf[...].astype(BF16), cmat_ref[cg]) + dskip_ref[:, cols] * ucls.pop(cg)
        z_ref[:, cols] = jax.nn.gelu(y, approximate=True)
        if cg % 2 == 1:
            pair = slice((cg - 1) * LANES, (cg + 1) * LANES)
            part = _dot(z_ref[:, pair].astype(BF16), wglu_ref[pair, :])
            gate = part if gate is None else gate + part

    outp = z_ref[...] * _sigmoid(gate + bglu_ref[...])
    for cg in range(S5_CLUSTERS):
        cols = slice(cg * LANES, (cg + 1) * LANES)
        for i in range(S5_SEG):
            outp_ref[cg, pl.ds(i, SUBLANES, stride=S5_SEG), :] = outp[i * SUBLANES:(i + 1) * SUBLANES, cols]
        o_ref[:, cols] = outp_ref[cg].astype(BF16)


def _s5_call(p_main, bmat, cmat, a_tab, aseg_tab, d_skip, w_glu, b_glu):
    l = p_main.shape[0]
    w = SSM_WIDTH
    full = lambda *shape: pl.BlockSpec(shape, lambda b: (0,) * len(shape))
    return pl.pallas_call(
        _s5_kernel,
        grid=(l // S5_BLOCK,),
        in_specs=[pl.BlockSpec((S5_BLOCK, LANES), functools.partial(lambda cg, b: (b, COL_U // LANES + cg), cg))
                  for cg in range(S5_CLUSTERS)] + [
                  full(S5_CLUSTERS, LANES, 2 * S5_CSTATE),
                  full(S5_CLUSTERS, 2 * S5_CSTATE, LANES),
                  full(S5_CLUSTERS, 2, S5_CSTATE),
                  full(S5_CLUSTERS, 2, S5_CSTATE),
                  full(1, w), full(w, w), full(1, w)],
        out_specs=pl.BlockSpec((S5_BLOCK, w), lambda b: (b, 0)),
        out_shape=jax.ShapeDtypeStruct((l, w), BF16),
        scratch_shapes=[pltpu.VMEM((2, S5_BLOCK, 2 * S5_CSTATE), F32),
                        pltpu.VMEM((S5_BLOCK, w), F32),
                        pltpu.VMEM((S5_CLUSTERS, SUBLANES, 2 * S5_CSTATE), F32),
                        pltpu.VMEM((S5_CLUSTERS, S5_BLOCK, LANES), F32)],
        compiler_params=_params(("arbitrary",)),
        name="s5_mixer",
    )(*([p_main] * S5_CLUSTERS), bmat, cmat, a_tab, aseg_tab, d_skip, w_glu, b_glu)


def _s5_tables(lam_re, lam_im, log_dt, b_re, b_im, c_re, c_im):
    lr, li = lam_re.astype(F32), lam_im.astype(F32)
    dt = jnp.exp(log_dt.astype(F32))[:, None]
    mag = jnp.exp(lr * dt)
    ang = li * dt
    ab_re, ab_im = mag * jnp.cos(ang), mag * jnp.sin(ang)
    den = lr * lr + li * li
    nr, ni = ab_re - 1.0, ab_im
    f_re = (nr * lr + ni * li) / den
    f_im = (ni * lr - nr * li) / den
    br, bi = b_re.astype(F32), b_im.astype(F32)
    bb_re = f_re[..., None] * br - f_im[..., None] * bi
    bb_im = f_re[..., None] * bi + f_im[..., None] * br
    gpc = LANES // SSM_GROUP
    eye = jnp.eye(gpc, dtype=F32)

    def in_block(bb):
        t = bb.reshape(S5_CLUSTERS, gpc, SSM_STATE, SSM_GROUP)
        m = jnp.einsum('cgph,gk->cghkp', t, eye)
        return m.reshape(S5_CLUSTERS, gpc * SSM_GROUP, gpc * SSM_STATE)

    def out_block(cc):
        t = cc.reshape(S5_CLUSTERS, gpc, SSM_GROUP, SSM_STATE)
        m = jnp.einsum('cghp,gk->cgpkh', t, eye)
        return m.reshape(S5_CLUSTERS, gpc * SSM_STATE, gpc * SSM_GROUP)

    bmat = jnp.concatenate([in_block(bb_re), in_block(bb_im)], axis=-1).astype(BF16)
    cmat = jnp.concatenate([out_block(c_re.astype(F32)), -out_block(c_im.astype(F32))], axis=1).astype(BF16)

    def tab(re, im):
        return jnp.stack([re.reshape(S5_CLUSTERS, S5_CSTATE), im.reshape(S5_CLUSTERS, S5_CSTATE)], axis=1)

    a_tab = tab(ab_re, ab_im)
    pr, pi = ab_re, ab_im
    for _ in range(int(math.log2(S5_SEG))):
        pr, pi = pr * pr - pi * pi, 2.0 * pr * pi
    return bmat, cmat, a_tab, tab(pr, pi)


def _tri_inverse_batch(ms, row, col):
    blk = 16

    def block_id(idx, size):
        return jnp.right_shift(idx, int(math.log2(size)))

    same = block_id(row, blk) == block_id(col, blk)
    eye = jnp.where(row == col, 1.0, 0.0)
    ps = [jnp.where(same, -m, 0.0) for m in ms]
    xs = [eye + p for p in ps]
    for _ in range(3):
        pbs = [p.astype(BF16) for p in ps]
        ps = [_dot(pb, pb) for pb in pbs]
        xs = [x + _dot(p.astype(BF16), x.astype(BF16)) for p, x in zip(ps, xs)]
    while blk < CHUNK:
        join = jnp.where(block_id(row, 2 * blk) == block_id(col, 2 * blk),
                         jnp.where(block_id(row, blk) != block_id(col, blk), 1.0, 0.0), 0.0)
        cbs = [(m * join).astype(BF16) for m in ms]
        xbs = [x.astype(BF16) for x in xs]
        ts = [_dot(xb, cb).astype(BF16) for xb, cb in zip(xbs, cbs)]
        xs = [x - _dot(t, xb) for x, t, xb in zip(xs, ts, xbs)]
        blk *= 2
    return xs


def _gdn_kernel(q_ref, k_ref, v_ref, z_ref, ps_ref, pst_ref, cw_ref, rowp_ref, colp_ref, nw_ref, o_ref,
                s_ref, tq_ref, tk_ref, tv_ref):
    r = GDN_BLOCK
    nchunks = r // CHUNK
    heads = range(GDN_HEADS)

    @pl.when(pl.program_id(0) == 0)
    def _():
        s_ref[...] = jnp.zeros_like(s_ref)
        tq_ref[...] = jnp.zeros_like(tq_ref)
        tk_ref[...] = jnp.zeros_like(tk_ref)
        tv_ref[...] = jnp.zeros_like(tv_ref)

    def conv_silu(x_ref, tail_ref, col0):
        x = x_ref[...]
        ext = jnp.concatenate([tail_ref[...], x], axis=0)
        w = cw_ref[:, col0:col0 + GDN_KDIM]
        y = w[CONV_WIDTH - 1:CONV_WIDTH, :] * x
        for j in range(1, CONV_WIDTH):
            y = y + w[CONV_WIDTH - 1 - j:CONV_WIDTH - j, :] * ext[SUBLANES - j:SUBLANES - j + r, :]
        tail_ref[...] = x[r - SUBLANES:r, :]
        return _silu(y)

    q = conv_silu(q_ref, tq_ref, 0)
    k = conv_silu(k_ref, tk_ref, GDN_KDIM)
    v = conv_silu(v_ref, tv_ref, 2 * GDN_KDIM)

    ps = ps_ref[...]
    rowi = lax.broadcasted_iota(jnp.int32, (r, LANES), 0)
    g_all = rowp_ref[0:1, :] * _softplus(ps + rowp_ref[1:2, :])
    for sh in (1, 2, 4, 8, 16, 32):
        g_all = g_all + jnp.where((rowi & (CHUNK - 1)) >= sh, pltpu.roll(g_all, sh, 0), 0.0)
    beta_all = _sigmoid(ps)
    pst = pst_ref[...]
    lane_t = lax.broadcasted_iota(jnp.int32, (SMALL_ROWS, r), 1)
    g_t = colp_ref[:, 0:1] * _softplus(pst + colp_ref[:, 1:2])
    for sh in (1, 2, 4, 8, 16, 32):
        g_t = g_t + jnp.where((lane_t & (CHUNK - 1)) >= sh, pltpu.roll(g_t, sh, 1), 0.0)

    row = lax.broadcasted_iota(jnp.int32, (CHUNK, CHUNK), 0)
    col = lax.broadcasted_iota(jnp.int32, (CHUNK, CHUNK), 1)
    causal = row >= col
    strict = row > col
    nw = nw_ref[...]

    qn, kn, vh = [], [], []
    for h in heads:
        hl = slice(h * GDN_DK, (h + 1) * GDN_DK)
        qh, kh = q[:, hl], k[:, hl]
        qn.append(qh * lax.rsqrt(jnp.sum(qh * qh, axis=-1, keepdims=True) + NORM_EPS) * (GDN_DK ** -0.5))
        kn.append(kh * lax.rsqrt(jnp.sum(kh * kh, axis=-1, keepdims=True) + NORM_EPS))
        vh.append(v[:, hl])

    items = [(c, h) for c in range(nchunks) for h in heads]
    it = {}
    for (c, h) in items:
        rows = slice(c * CHUNK, (c + 1) * CHUNK)
        d = {}
        d['q'], d['k'], d['v'] = qn[h][rows], kn[h][rows], vh[h][rows]
        d['beta'] = beta_all[rows, h:h + 1]
        gcc = g_all[rows, GDN_HEADS + h:GDN_HEADS + h + 1]
        gcr = g_t[GDN_HEADS + h:GDN_HEADS + h + 1, c * CHUNK:(c + 1) * CHUNK]
        d['g_end'] = gcc[CHUNK - 1:CHUNK, :]
        d['eg'] = jnp.exp(gcc)
        d['etail'] = jnp.exp(d['g_end'] - gcc)
        d['decay'] = jnp.where(causal, jnp.exp(jnp.where(causal, gcc - gcr, 0.0)), 0.0)
        d['kb'] = d['k'] * d['beta']
        it[(c, h)] = d

    kqs = [_dot_nt(jnp.concatenate([it[i]['kb'], it[i]['q']], axis=0).astype(BF16), it[i]['k'].astype(BF16))
           for i in items]
    ms = [jnp.where(strict, kq[:CHUNK] * it[i]['decay'], 0.0) for kq, i in zip(kqs, items)]
    attns = [(kq[CHUNK:] * it[i]['decay']).astype(BF16) for kq, i in zip(kqs, items)]
    t_invs = _tri_inverse_batch(ms, row, col)
    uws = [_dot(t.astype(BF16),
                jnp.concatenate([it[i]['v'] * it[i]['beta'], it[i]['kb'] * it[i]['eg']], axis=1).astype(BF16))
           for t, i in zip(t_invs, items)]
    for i, uw, attn in zip(items, uws, attns):
        d = it[i]
        d['u'] = uw[:, :GDN_DV]
        d['wqd'] = jnp.concatenate([uw[:, GDN_DV:], d['q'] * d['eg']], axis=0).astype(BF16)
        d['kt'] = (d['k'] * d['etail']).astype(BF16)
        d['attn'] = attn

    states = [s_ref[h] for h in heads]
    for c in range(nchunks):
        rows = slice(c * CHUNK, (c + 1) * CHUNK)
        wqs = [_dot(it[(c, h)]['wqd'], states[h].astype(BF16)) for h in heads]
        v_news = [it[(c, h)]['u'] - wqs[h][:CHUNK] for h in heads]
        v_bfs = [vn.astype(BF16) for vn in v_news]
        o_cs = [wqs[h][CHUNK:] + _dot(it[(c, h)]['attn'], v_bfs[h]) for h in heads]
        states = [states[h] * jnp.exp(it[(c, h)]['g_end']) + lax.dot_general(
            it[(c, h)]['kt'], v_bfs[h], (((0,), (0,)), ((), ())), preferred_element_type=F32) for h in heads]
        for h in heads:
            hl = slice(h * GDN_DV, (h + 1) * GDN_DV)
            o_c = o_cs[h]
            o_n = o_c * lax.rsqrt(jnp.mean(o_c * o_c, axis=-1, keepdims=True) + NORM_EPS) * nw
            o_ref[rows, hl] = (o_n * _silu(z_ref[rows, hl])).astype(BF16)
    for h in heads:
        s_ref[h] = states[h]


def _gdn_call(p_main, p_small, p_small_t, conv_w, rowp, colp, norm_w):
    l = p_main.shape[0]
    r = GDN_BLOCK
    wide = lambda base: pl.BlockSpec((r, GDN_KDIM), lambda b: (b, base // GDN_KDIM))
    full = lambda *shape: pl.BlockSpec(shape, lambda b: (0,) * len(shape))
    return pl.pallas_call(
        _gdn_kernel,
        grid=(l // r,),
        in_specs=[wide(COL_Q), wide(COL_K), wide(COL_V), wide(COL_Z),
                  pl.BlockSpec((r, LANES), lambda b: (b, 0)),
                  pl.BlockSpec((SMALL_ROWS, r), lambda b: (0, b)),
                  full(CONV_WIDTH, 2 * GDN_KDIM + GDN_VDIM),
                  full(2, LANES), full(SMALL_ROWS, 2), full(1, GDN_DV)],
        out_specs=pl.BlockSpec((r, GDN_VDIM), lambda b: (b, 0)),
        out_shape=jax.ShapeDtypeStruct((l, GDN_VDIM), BF16),
        scratch_shapes=[pltpu.VMEM((GDN_HEADS, GDN_DK, GDN_DV), F32),
                        pltpu.VMEM((SUBLANES, GDN_KDIM), F32),
                        pltpu.VMEM((SUBLANES, GDN_KDIM), F32),
                        pltpu.VMEM((SUBLANES, GDN_VDIM), F32)],
        compiler_params=_params(("arbitrary",)),
        name="gated_deltanet",
    )(p_main, p_main, p_main, p_main, p_small, p_small_t, conv_w, rowp, colp, norm_w)


def _merge_kernel(ys_ref, yg_ref, gs_ref, gg_ref, x_ref, wus_ref, wug_ref, wo_ref,
                  gm_ref, g1_ref, b1_ref, sc_ref, sh_ref, xo_ref, ho_ref):
    tm = ys_ref.shape[0]
    halves = [slice(k * (tm // 2), (k + 1) * (tm // 2)) for k in range(2)]
    ab = [(_dot(ys_ref[r, :], wus_ref[...]), _dot(yg_ref[r, :], wug_ref[...])) for r in halves]
    ms = [(_sigmoid(gs_ref[r, :]) * a + _sigmoid(gg_ref[r, :]) * b).astype(BF16) for r, (a, b) in zip(halves, ab)]
    ys = [_dot(m, wo_ref[...]) for m in ms]
    for r, y in zip(halves, ys):
        xn = _ln(DEEPNORM_ALPHA * x_ref[r, :] + gm_ref[...] * y) * g1_ref[...] + b1_ref[...]
        xo_ref[r, :] = xn
        ho_ref[r, :] = (_ln(xn) * (1.0 + sc_ref[...]) + sh_ref[...]).astype(BF16)


def _merge_call(ys, yg, p_main, x, w_us, w_ug, w_out, g_m, ln_g, ln_b, sc_f, sh_f):
    l, d = x.shape
    tm = 256
    vec = pl.BlockSpec((1, d), lambda i: (0, 0))
    once = lambda shape: pl.BlockSpec(shape, lambda i: (0, 0), pipeline_mode=pl.Buffered(1))
    return pl.pallas_call(
        _merge_kernel,
        grid=(l // tm,),
        in_specs=[pl.BlockSpec((tm, SSM_WIDTH), lambda i: (i, 0)),
                  pl.BlockSpec((tm, GDN_VDIM), lambda i: (i, 0)),
                  pl.BlockSpec((tm, d), lambda i: (i, COL_GATE_S // d)),
                  pl.BlockSpec((tm, d), lambda i: (i, COL_GATE_G // d)),
                  pl.BlockSpec((tm, d), lambda i: (i, 0)),
                  once((SSM_WIDTH, d)), once((GDN_VDIM, d)), once((d, d)),
                  vec, vec, vec, vec, vec],
        out_specs=[pl.BlockSpec((tm, d), lambda i: (i, 0)),
                   pl.BlockSpec((tm, d), lambda i: (i, 0))],
        out_shape=[jax.ShapeDtypeStruct((l, d), F32), jax.ShapeDtypeStruct((l, d), BF16)],
        compiler_params=_params(("arbitrary",)),
        name="merge_out_ln1",
    )(ys, yg, p_main, p_main, x, w_us, w_ug, w_out, g_m, ln_g, ln_b, sc_f, sh_f)


def _ffn_kernel(with_next, h_ref, wg_ref, wu_ref, wo_ref, x_ref, gf_ref, g2_ref, b2_ref, sc_ref, sh_ref,
                xo_ref, *rest):
    ho_ref, acc_ref = rest if with_next else (None, rest[0])
    j = pl.program_id(1)

    @pl.when(j == 0)
    def _():
        acc_ref[...] = jnp.zeros_like(acc_ref)

    h = h_ref[...]
    gate = _dot(h, wg_ref[...])
    up = _dot(h, wu_ref[...])
    acc_ref[...] += _dot((_silu(gate) * up).astype(BF16), wo_ref[...])

    @pl.when(j == pl.num_programs(1) - 1)
    def _():
        xn = _ln(DEEPNORM_ALPHA * x_ref[...] + gf_ref[...] * acc_ref[...]) * g2_ref[...] + b2_ref[...]
        xo_ref[...] = xn
        if with_next:
            ho_ref[...] = (_ln(xn) * (1.0 + sc_ref[...]) + sh_ref[...]).astype(BF16)


def _ffn_call(h, w_in, w_out, x, g_f, ln_g, ln_b, sc_n, sh_n, with_next):
    l, d = x.shape
    f = w_out.shape[0]
    tm, th = 512, 512
    nj = f // th
    vec = pl.BlockSpec((1, d), lambda i, j: (0, 0))
    row_tile = pl.BlockSpec((tm, d), lambda i, j: (i, 0))
    out_specs = [row_tile] + ([row_tile] if with_next else [])
    out_shape = [jax.ShapeDtypeStruct((l, d), F32)] + ([jax.ShapeDtypeStruct((l, d), BF16)] if with_next else [])
    outs = pl.pallas_call(
        functools.partial(_ffn_kernel, with_next),
        grid=(l // tm, nj),
        in_specs=[row_tile,
                  pl.BlockSpec((d, th), lambda i, j: (0, j)),
                  pl.BlockSpec((d, th), lambda i, j: (0, nj + j)),
                  pl.BlockSpec((th, d), lambda i, j: (j, 0)),
                  row_tile,
                  vec, vec, vec, vec, vec],
        out_specs=out_specs,
        out_shape=out_shape,
        scratch_shapes=[pltpu.VMEM((tm, d), F32)],
        compiler_params=_params(("arbitrary", "arbitrary")),
        name="ffn_ln2",
    )(h, w_in, w_in, w_out, x, g_f, ln_g, ln_b, sc_n, sh_n)
    return (outs[0], outs[1]) if with_next else (outs[0], None)


def _split_w_in(w):
    o_small = SSM_WIDTH + 2 * GDN_KDIM + 2 * GDN_VDIM
    o_gates = o_small + SMALL_ROWS
    return (w[:, o_gates:o_gates + 2 * D_MODEL].astype(BF16), w[:, :o_small].astype(BF16),
            w[:, o_small:o_small + LANES].astype(BF16))


def kernel(x, c, w_ada, b_ada, w_in, ssm_lam_re, ssm_lam_im, ssm_log_dt, ssm_b_re, ssm_b_im, ssm_c_re, ssm_c_im, ssm_d, ssm_w_glu, ssm_b_glu, gdn_conv_w, gdn_a_log, gdn_dt_bias, gdn_norm_w, w_up_ssm, w_up_gdn, w_mix_out, ln1_g, ln1_b, ffn_w_in, ffn_w_out, ln2_g, ln2_b):
    bsz, seq, d = x.shape
    assert bsz == 1 and d == D_MODEL and seq % 1024 == 0
    xs = x.reshape(seq, d).astype(F32)
    mod = _ada_call(c.astype(F32), w_ada.astype(F32), b_ada.astype(F32))
    sh_m, sc_m, g_m, sh_f, sc_f, g_f = [mod[:, :, i * d:(i + 1) * d] for i in range(6)]

    h = _lnmod_call(xs, sc_m[0], sh_m[0])
    for l in range(DEPTH):
        w_gates, w_act, w_small = _split_w_in(w_in[l])
        p_main, p_small, p_small_t = _proj_call(h, w_gates, w_act, w_small)

        bmat, cmat, a_tab, aseg_tab = _s5_tables(ssm_lam_re[l], ssm_lam_im[l], ssm_log_dt[l], ssm_b_re[l],
                                                 ssm_b_im[l], ssm_c_re[l], ssm_c_im[l])
        ys = _s5_call(p_main, bmat, cmat, a_tab, aseg_tab, ssm_d[l].astype(F32).reshape(1, -1),
                      ssm_w_glu[l].astype(BF16), ssm_b_glu[l].astype(F32).reshape(1, -1))

        neg_a = -jnp.exp(gdn_a_log[l].astype(F32))
        dtb = gdn_dt_bias[l].astype(F32)
        zeros = jnp.zeros((GDN_HEADS,), F32)
        decay_a = jnp.concatenate([zeros, neg_a])
        decay_b = jnp.concatenate([zeros, dtb])
        rowp = jnp.pad(jnp.stack([decay_a, decay_b]), ((0, 0), (0, LANES - SMALL_ROWS)))
        colp = jnp.stack([decay_a, decay_b], axis=1)
        yg = _gdn_call(p_main, p_small, p_small_t, gdn_conv_w[l].astype(F32), rowp, colp,
                       gdn_norm_w[l].astype(F32).reshape(1, -1))

        xs, h2 = _merge_call(ys, yg, p_main, xs, w_up_ssm[l].astype(BF16), w_up_gdn[l].astype(BF16),
                             w_mix_out[l].astype(BF16), g_m[l], ln1_g[l].astype(F32).reshape(1, -1),
                             ln1_b[l].astype(F32).reshape(1, -1), sc_f[l], sh_f[l])
        nxt = min(l + 1, DEPTH - 1)
        xs, h = _ffn_call(h2, ffn_w_in[l].astype(BF16), ffn_w_out[l].astype(BF16), xs, g_f[l],
                          ln2_g[l].astype(F32).reshape(1, -1), ln2_b[l].astype(F32).reshape(1, -1),
                          sc_m[nxt], sh_m[nxt], with_next=l + 1 < DEPTH)
    return xs.reshape(bsz, seq, d)
```

```python
import functools
import math

import jax
import jax.numpy as jnp
from jax import lax
from jax.experimental import pallas as pl
from jax.experimental.pallas import tpu as pltpu

F32 = jnp.float32
BF16 = jnp.bfloat16

D_MODEL = 2048
DEPTH = 2
CHUNK = 64
SSM_WIDTH = D_MODEL // 2
SSM_GROUP = 16
SSM_GROUPS = SSM_WIDTH // SSM_GROUP
SSM_STATE = 64
GDN_HEADS = 8
GDN_DK = 128
GDN_DV = 128
GDN_KDIM = GDN_HEADS * GDN_DK
GDN_VDIM = GDN_HEADS * GDN_DV
CONV_WIDTH = 4
FFN_HIDDEN = -(-8 * D_MODEL // (3 * 256)) * 256
DEEPNORM_ALPHA = (2 * DEPTH) ** 0.25
LN_EPS = 1e-5
NORM_EPS = 1e-6

SUBLANES = 8
LANES = 128
VMEM_LIMIT_BYTES = 56 * 1024 * 1024

COL_GATE_S = 0
COL_GATE_G = D_MODEL
COL_U = 2 * D_MODEL
COL_Q = COL_U + SSM_WIDTH
COL_K = COL_Q + GDN_KDIM
COL_V = COL_K + GDN_KDIM
COL_Z = COL_V + GDN_VDIM
PROJ_MAIN = COL_Z + GDN_VDIM
SMALL_ROWS = 2 * GDN_HEADS

S5_SEG = 64
S5_BLOCK = SUBLANES * S5_SEG
S5_CLUSTERS = SSM_WIDTH // LANES
S5_CSTATE = (LANES // SSM_GROUP) * SSM_STATE

GDN_BLOCK = 128


def _params(sem):
    return pltpu.CompilerParams(dimension_semantics=sem, vmem_limit_bytes=VMEM_LIMIT_BYTES)


def _ln(x):
    mu = jnp.mean(x, axis=-1, keepdims=True)
    xc = x - mu
    var = jnp.mean(xc * xc, axis=-1, keepdims=True)
    return xc * lax.rsqrt(var + LN_EPS)


def _sigmoid(x):
    return 1.0 / (1.0 + jnp.exp(-x))


def _silu(x):
    return x * _sigmoid(x)


def _softplus(x):
    return jnp.maximum(x, 0.0) + jnp.log(1.0 + jnp.exp(-jnp.abs(x)))


def _dot(a, b):
    return jnp.dot(a, b, preferred_element_type=F32)


def _dot_nt(a, b):
    return lax.dot_general(a, b, (((1,), (1,)), ((), ())), preferred_element_type=F32)


def _ada_kernel(c_ref, w_ref, b_ref, o_ref):
    c = c_ref[...]
    s = _silu(c).astype(BF16)
    o_ref[0] = _dot(s, w_ref[0].astype(BF16)) + b_ref[0]


def _ada_call(c, w_ada, b_ada):
    nl, d, n = w_ada.shape
    tn = 1024
    c8 = jnp.broadcast_to(c, (SUBLANES, d))
    out = pl.pallas_call(
        _ada_kernel,
        grid=(nl, n // tn),
        in_specs=[pl.BlockSpec((SUBLANES, d), lambda l, j: (0, 0)),
                  pl.BlockSpec((1, d, tn), lambda l, j: (l, 0, j)),
                  pl.BlockSpec((1, 1, tn), lambda l, j: (l, 0, j))],
        out_specs=pl.BlockSpec((1, SUBLANES, tn), lambda l, j: (l, 0, j)),
        out_shape=jax.ShapeDtypeStruct((nl, SUBLANES, n), F32),
        compiler_params=_params(("arbitrary", "arbitrary")),
        name="adaln_mod",
    )(c8, w_ada, b_ada.reshape(nl, 1, n))
    return out[:, 0:1, :]


def _lnmod_kernel(x_ref, sc_ref, sh_ref, h_ref):
    y = _ln(x_ref[...])
    h_ref[...] = (y * (1.0 + sc_ref[...]) + sh_ref[...]).astype(BF16)


def _lnmod_call(x, sc, sh):
    l, d = x.shape
    tm = 512
    return pl.pallas_call(
        _lnmod_kernel,
        grid=(l // tm,),
        in_specs=[pl.BlockSpec((tm, d), lambda i: (i, 0)),
                  pl.BlockSpec((1, d), lambda i: (0, 0)),
                  pl.BlockSpec((1, d), lambda i: (0, 0))],
        out_specs=pl.BlockSpec((tm, d), lambda i: (i, 0)),
        out_shape=jax.ShapeDtypeStruct((l, d), BF16),
        compiler_params=_params(("arbitrary",)),
        name="ln0_mod",
    )(x, sc, sh)


PROJ_TN = 1024
PROJ_GATE_TILES = 2 * D_MODEL // PROJ_TN


def _proj_kernel(h_ref, wg_ref, wa_ref, ws_ref, wst_ref, pm_ref, ps_ref, pst_ref):
    j = pl.program_id(1)
    h = h_ref[...]

    @pl.when(j < PROJ_GATE_TILES)
    def _():
        pm_ref[...] = _dot(h, wg_ref[...])

    @pl.when(j >= PROJ_GATE_TILES)
    def _():
        pm_ref[...] = _dot(h, wa_ref[...])

    @pl.when(j == 0)
    def _():
        ps_ref[...] = _dot(h, ws_ref[...])
        pst_ref[...] = _dot_nt(wst_ref[...], h)


def _proj_call(layer, h, w_gates, w_all, w_small, w_small_t):
    l, d = h.shape
    tm, tn = 1024, PROJ_TN
    ng = PROJ_GATE_TILES
    n = PROJ_MAIN
    return pl.pallas_call(
        _proj_kernel,
        grid=(l // tm, n // tn),
        in_specs=[pl.BlockSpec((tm, d), lambda i, j: (i, 0)),
                  pl.BlockSpec((None, d, tn), lambda i, j: (layer, 0, jnp.minimum(j, ng - 1))),
                  pl.BlockSpec((None, d, tn), lambda i, j: (layer, 0, jnp.maximum(j - ng, 0))),
                  pl.BlockSpec((None, d, LANES), lambda i, j: (layer, 0, 0)),
                  pl.BlockSpec((None, SMALL_ROWS, d), lambda i, j: (layer, 0, 0))],
        out_specs=[pl.BlockSpec((tm, tn), lambda i, j: (i, j)),
                   pl.BlockSpec((tm, LANES), lambda i, j: (i, 0)),
                   pl.BlockSpec((SMALL_ROWS, tm), lambda i, j: (0, i))],
        out_shape=[jax.ShapeDtypeStruct((l, n), F32),
                   jax.ShapeDtypeStruct((l, LANES), F32),
                   jax.ShapeDtypeStruct((SMALL_ROWS, l), F32)],
        compiler_params=_params(("arbitrary", "arbitrary")),
        name="in_proj",
    )(h, w_gates, w_all, w_small, w_small_t)


def _s5_kernel(*refs):
    u_refs = refs[:S5_CLUSTERS]
    (bmat_ref, cmat_ref, a_ref, aseg_ref, dskip_ref, wglu_ref, bglu_ref, o_ref,
     bu_ref, z_ref, carry_ref, outp_ref) = refs[S5_CLUSTERS:]

    @pl.when(pl.program_id(0) == 0)
    def _():
        carry_ref[...] = jnp.zeros_like(carry_ref)

    sub = lax.broadcasted_iota(jnp.int32, (SUBLANES, S5_CSTATE), 0)
    half = S5_CSTATE
    zero = jnp.zeros((SUBLANES, half), F32)

    def in_proj(cg):
        ucl = jnp.concatenate(
            [u_refs[cg][pl.ds(i, SUBLANES, stride=S5_SEG), :] for i in range(S5_SEG)], axis=0)
        bu_ref[cg % 2] = _dot(ucl.astype(BF16), bmat_ref[cg])
        return ucl

    ucls = {0: in_proj(0)}
    gate = None
    for cg in range(S5_CLUSTERS):
        cols = slice(cg * LANES, (cg + 1) * LANES)
        buf = bu_ref.at[cg % 2]
        if cg + 1 < S5_CLUSTERS:
            ucls[cg + 1] = in_proj(cg + 1)
        ar = jnp.broadcast_to(a_ref[cg, 0:1, :], (SUBLANES, half))
        ai = jnp.broadcast_to(a_ref[cg, 1:2, :], (SUBLANES, half))

        def step(i, xr, xi):
            rows = slice(i * SUBLANES, (i + 1) * SUBLANES)
            bur, bui = buf[rows, 0:half], buf[rows, half:2 * half]
            return ar * xr - ai * xi + bur, ar * xi + ai * xr + bui

        er, ei = zero, zero
        for i in range(S5_SEG):
            er, ei = step(i, er, ei)

        pr = jnp.broadcast_to(aseg_ref[cg, 0:1, :], (SUBLANES, half))
        pi = jnp.broadcast_to(aseg_ref[cg, 1:2, :], (SUBLANES, half))
        cr = carry_ref[cg, :, 0:half]
        ci = carry_ref[cg, :, half:2 * half]
        xr, xi = zero, zero
        for s in range(SUBLANES):
            xr = jnp.where(sub == s, cr, xr)
            xi = jnp.where(sub == s, ci, xi)
            esr = jnp.broadcast_to(er[s:s + 1, :], (SUBLANES, half))
            esi = jnp.broadcast_to(ei[s:s + 1, :], (SUBLANES, half))
            cr, ci = pr * cr - pi * ci + esr, pr * ci + pi * cr + esi
        carry_ref[cg, :, 0:half] = cr
        carry_ref[cg, :, half:2 * half] = ci

        for i in range(S5_SEG):
            xr, xi = step(i, xr, xi)
            rows = slice(i * SUBLANES, (i + 1) * SUBLANES)
            buf[rows, 0:half] = xr
            buf[rows, half:2 * half] = xi

        y = _dot(buf[...].astype(BF16), cmat_ref[cg]) + dskip_ref[:, cols] * ucls.pop(cg)
        z_ref[:, cols] = jax.nn.gelu(y, approximate=True)
        if cg % 2 == 1:
            pair = slice((cg - 1) * LANES, (cg + 1) * LANES)
            part = _dot(z_ref[:, pair].astype(BF16), wglu_ref[pair, :])
            gate = part if gate is None else gate + part

    outp = z_ref[...] * _sigmoid(gate + bglu_ref[...])
    for cg in range(S5_CLUSTERS):
        cols = slice(cg * LANES, (cg + 1) * LANES)
        for i in range(S5_SEG):
            outp_ref[cg, pl.ds(i, SUBLANES, stride=S5_SEG), :] = outp[i * SUBLANES:(i + 1) * SUBLANES, cols]
        o_ref[:, cols] = outp_ref[cg].astype(BF16)


def _s5_call(layer, p_main, bmat, cmat, a_tab, aseg_tab, d_skip, w_glu, b_glu):
    l = p_main.shape[0]
    w = SSM_WIDTH
    full = lambda *shape: pl.BlockSpec(shape, lambda b: (0,) * len(shape))
    return pl.pallas_call(
        _s5_kernel,
        grid=(l // S5_BLOCK,),
        in_specs=[pl.BlockSpec((S5_BLOCK, LANES), functools.partial(lambda cg, b: (b, COL_U // LANES + cg), cg))
                  for cg in range(S5_CLUSTERS)] + [
                  full(S5_CLUSTERS, LANES, 2 * S5_CSTATE),
                  full(S5_CLUSTERS, 2 * S5_CSTATE, LANES),
                  full(S5_CLUSTERS, 2, S5_CSTATE),
                  full(S5_CLUSTERS, 2, S5_CSTATE),
                  full(1, w), pl.BlockSpec((None, w, w), lambda b: (layer, 0, 0)), full(1, w)],
        out_specs=pl.BlockSpec((S5_BLOCK, w), lambda b: (b, 0)),
        out_shape=jax.ShapeDtypeStruct((l, w), BF16),
        scratch_shapes=[pltpu.VMEM((2, S5_BLOCK, 2 * S5_CSTATE), F32),
                        pltpu.VMEM((S5_BLOCK, w), F32),
                        pltpu.VMEM((S5_CLUSTERS, SUBLANES, 2 * S5_CSTATE), F32),
                        pltpu.VMEM((S5_CLUSTERS, S5_BLOCK, LANES), F32)],
        compiler_params=_params(("arbitrary",)),
        name="s5_mixer",
    )(*([p_main] * S5_CLUSTERS), bmat, cmat, a_tab, aseg_tab, d_skip, w_glu, b_glu)


def _s5_tables(lam_re, lam_im, log_dt, b_re, b_im, c_re, c_im):
    lr, li = lam_re.astype(F32), lam_im.astype(F32)
    dt = jnp.exp(log_dt.astype(F32))[:, None]
    mag = jnp.exp(lr * dt)
    ang = li * dt
    ab_re, ab_im = mag * jnp.cos(ang), mag * jnp.sin(ang)
    den = lr * lr + li * li
    nr, ni = ab_re - 1.0, ab_im
    f_re = (nr * lr + ni * li) / den
    f_im = (ni * lr - nr * li) / den
    br, bi = b_re.astype(F32), b_im.astype(F32)
    bb_re = f_re[..., None] * br - f_im[..., None] * bi
    bb_im = f_re[..., None] * bi + f_im[..., None] * br
    gpc = LANES // SSM_GROUP
    eye = jnp.eye(gpc, dtype=F32)

    def in_block(bb):
        t = bb.reshape(S5_CLUSTERS, gpc, SSM_STATE, SSM_GROUP)
        m = jnp.einsum('cgph,gk->cghkp', t, eye)
        return m.reshape(S5_CLUSTERS, gpc * SSM_GROUP, gpc * SSM_STATE)

    def out_block(cc):
        t = cc.reshape(S5_CLUSTERS, gpc, SSM_GROUP, SSM_STATE)
        m = jnp.einsum('cghp,gk->cgpkh', t, eye)
        return m.reshape(S5_CLUSTERS, gpc * SSM_STATE, gpc * SSM_GROUP)

    bmat = jnp.concatenate([in_block(bb_re), in_block(bb_im)], axis=-1).astype(BF16)
    cmat = jnp.concatenate([out_block(c_re.astype(F32)), -out_block(c_im.astype(F32))], axis=1).astype(BF16)

    def tab(re, im):
        return jnp.stack([re.reshape(S5_CLUSTERS, S5_CSTATE), im.reshape(S5_CLUSTERS, S5_CSTATE)], axis=1)

    a_tab = tab(ab_re, ab_im)
    pr, pi = ab_re, ab_im
    for _ in range(int(math.log2(S5_SEG))):
        pr, pi = pr * pr - pi * pi, 2.0 * pr * pi
    return bmat, cmat, a_tab, tab(pr, pi)


def _tri_inverse_batch(ms, row, col):
    blk = 16

    def block_id(idx, size):
        return jnp.right_shift(idx, int(math.log2(size)))

    same = block_id(row, blk) == block_id(col, blk)
    eye = jnp.where(row == col, 1.0, 0.0)
    ps = [jnp.where(same, -m, 0.0) for m in ms]
    xs = [eye + p for p in ps]
    for _ in range(3):
        pbs = [p.astype(BF16) for p in ps]
        ps = [_dot(pb, pb) for pb in pbs]
        xs = [x + _dot(p.astype(BF16), x.astype(BF16)) for p, x in zip(ps, xs)]
    while blk < CHUNK:
        join = jnp.where(block_id(row, 2 * blk) == block_id(col, 2 * blk),
                         jnp.where(block_id(row, blk) != block_id(col, blk), 1.0, 0.0), 0.0)
        cbs = [(m * join).astype(BF16) for m in ms]
        xbs = [x.astype(BF16) for x in xs]
        ts = [_dot(xb, cb).astype(BF16) for xb, cb in zip(xbs, cbs)]
        xs = [x - _dot(t, xb) for x, t, xb in zip(xs, ts, xbs)]
        blk *= 2
    return xs


def _gdn_kernel(q_ref, k_ref, v_ref, z_ref, ps_ref, pst_ref, cw_ref, rowp_ref, colp_ref, nw_ref, o_ref,
                s_ref, tq_ref, tk_ref, tv_ref):
    r = GDN_BLOCK
    nchunks = r // CHUNK
    heads = range(GDN_HEADS)

    @pl.when(pl.program_id(0) == 0)
    def _():
        s_ref[...] = jnp.zeros_like(s_ref)
        tq_ref[...] = jnp.zeros_like(tq_ref)
        tk_ref[...] = jnp.zeros_like(tk_ref)
        tv_ref[...] = jnp.zeros_like(tv_ref)

    def conv_silu(x_ref, tail_ref, col0):
        x = x_ref[...]
        ext = jnp.concatenate([tail_ref[...], x], axis=0)
        w = cw_ref[:, col0:col0 + GDN_KDIM]
        y = w[CONV_WIDTH - 1:CONV_WIDTH, :] * x
        for j in range(1, CONV_WIDTH):
            y = y + w[CONV_WIDTH - 1 - j:CONV_WIDTH - j, :] * ext[SUBLANES - j:SUBLANES - j + r, :]
        tail_ref[...] = x[r - SUBLANES:r, :]
        return _silu(y)

    q = conv_silu(q_ref, tq_ref, 0)
    k = conv_silu(k_ref, tk_ref, GDN_KDIM)
    v = conv_silu(v_ref, tv_ref, 2 * GDN_KDIM)

    ps = ps_ref[...]
    rowi = lax.broadcasted_iota(jnp.int32, (r, LANES), 0)
    g_all = rowp_ref[0:1, :] * _softplus(ps + rowp_ref[1:2, :])
    for sh in (1, 2, 4, 8, 16, 32):
        g_all = g_all + jnp.where((rowi & (CHUNK - 1)) >= sh, pltpu.roll(g_all, sh, 0), 0.0)
    beta_all = _sigmoid(ps)
    pst = pst_ref[...]
    lane_t = lax.broadcasted_iota(jnp.int32, (SMALL_ROWS, r), 1)
    g_t = colp_ref[:, 0:1] * _softplus(pst + colp_ref[:, 1:2])
    for sh in (1, 2, 4, 8, 16, 32):
        g_t = g_t + jnp.where((lane_t & (CHUNK - 1)) >= sh, pltpu.roll(g_t, sh, 1), 0.0)

    row = lax.broadcasted_iota(jnp.int32, (CHUNK, CHUNK), 0)
    col = lax.broadcasted_iota(jnp.int32, (CHUNK, CHUNK), 1)
    causal = row >= col
    strict = row > col
    nw = nw_ref[...]

    qn, kn, vh = [], [], []
    for h in heads:
        hl = slice(h * GDN_DK, (h + 1) * GDN_DK)
        qh, kh = q[:, hl], k[:, hl]
        qn.append(qh * lax.rsqrt(jnp.sum(qh * qh, axis=-1, keepdims=True) + NORM_EPS) * (GDN_DK ** -0.5))
        kn.append(kh * lax.rsqrt(jnp.sum(kh * kh, axis=-1, keepdims=True) + NORM_EPS))
        vh.append(v[:, hl])

    items = [(c, h) for c in range(nchunks) for h in heads]
    it = {}
    for (c, h) in items:
        rows = slice(c * CHUNK, (c + 1) * CHUNK)
        d = {}
        d['q'], d['k'], d['v'] = qn[h][rows], kn[h][rows], vh[h][rows]
        d['beta'] = beta_all[rows, h:h + 1]
        gcc = g_all[rows, GDN_HEADS + h:GDN_HEADS + h + 1]
        gcr = g_t[GDN_HEADS + h:GDN_HEADS + h + 1, c * CHUNK:(c + 1) * CHUNK]
        d['g_end'] = gcc[CHUNK - 1:CHUNK, :]
        d['eg'] = jnp.exp(gcc)
        d['etail'] = jnp.exp(d['g_end'] - gcc)
        d['decay'] = jnp.where(causal, jnp.exp(jnp.where(causal, gcc - gcr, 0.0)), 0.0)
        d['kb'] = d['k'] * d['beta']
        it[(c, h)] = d

    kqs = [_dot_nt(jnp.concatenate([it[i]['kb'], it[i]['q']], axis=0).astype(BF16), it[i]['k'].astype(BF16))
           for i in items]
    ms = [jnp.where(strict, kq[:CHUNK] * it[i]['decay'], 0.0) for kq, i in zip(kqs, items)]
    attns = [(kq[CHUNK:] * it[i]['decay']).astype(BF16) for kq, i in zip(kqs, items)]
    t_invs = _tri_inverse_batch(ms, row, col)
    uws = [_dot(t.astype(BF16),
                jnp.concatenate([it[i]['v'] * it[i]['beta'], it[i]['kb'] * it[i]['eg']], axis=1).astype(BF16))
           for t, i in zip(t_invs, items)]
    for i, uw, attn in zip(items, uws, attns):
        d = it[i]
        d['u'] = uw[:, :GDN_DV]
        d['wqd'] = jnp.concatenate([uw[:, GDN_DV:], d['q'] * d['eg']], axis=0).astype(BF16)
        d['kt'] = (d['k'] * d['etail']).astype(BF16)
        d['attn'] = attn

    states = [s_ref[h] for h in heads]
    for c in range(nchunks):
        rows = slice(c * CHUNK, (c + 1) * CHUNK)
        wqs = [_dot(it[(c, h)]['wqd'], states[h].astype(BF16)) for h in heads]
        v_news = [it[(c, h)]['u'] - wqs[h][:CHUNK] for h in heads]
        v_bfs = [vn.astype(BF16) for vn in v_news]
        o_cs = [wqs[h][CHUNK:] + _dot(it[(c, h)]['attn'], v_bfs[h]) for h in heads]
        states = [states[h] * jnp.exp(it[(c, h)]['g_end']) + lax.dot_general(
            it[(c, h)]['kt'], v_bfs[h], (((0,), (0,)), ((), ())), preferred_element_type=F32) for h in heads]
        for h in heads:
            hl = slice(h * GDN_DV, (h + 1) * GDN_DV)
            o_c = o_cs[h]
            o_n = o_c * lax.rsqrt(jnp.mean(o_c * o_c, axis=-1, keepdims=True) + NORM_EPS) * nw
            o_ref[rows, hl] = (o_n * _silu(z_ref[rows, hl])).astype(BF16)
    for h in heads:
        s_ref[h] = states[h]


def _gdn_call(p_main, p_small, p_small_t, conv_w, rowp, colp, norm_w):
    l = p_main.shape[0]
    r = GDN_BLOCK
    wide = lambda base: pl.BlockSpec((r, GDN_KDIM), lambda b: (b, base // GDN_KDIM))
    full = lambda *shape: pl.BlockSpec(shape, lambda b: (0,) * len(shape))
    return pl.pallas_call(
        _gdn_kernel,
        grid=(l // r,),
        in_specs=[wide(COL_Q), wide(COL_K), wide(COL_V), wide(COL_Z),
                  pl.BlockSpec((r, LANES), lambda b: (b, 0)),
                  pl.BlockSpec((SMALL_ROWS, r), lambda b: (0, b)),
                  full(CONV_WIDTH, 2 * GDN_KDIM + GDN_VDIM),
                  full(2, LANES), full(SMALL_ROWS, 2), full(1, GDN_DV)],
        out_specs=pl.BlockSpec((r, GDN_VDIM), lambda b: (b, 0)),
        out_shape=jax.ShapeDtypeStruct((l, GDN_VDIM), BF16),
        scratch_shapes=[pltpu.VMEM((GDN_HEADS, GDN_DK, GDN_DV), F32),
                        pltpu.VMEM((SUBLANES, GDN_KDIM), F32),
                        pltpu.VMEM((SUBLANES, GDN_KDIM), F32),
                        pltpu.VMEM((SUBLANES, GDN_VDIM), F32)],
        compiler_params=_params(("arbitrary",)),
        name="gated_deltanet",
    )(p_main, p_main, p_main, p_main, p_small, p_small_t, conv_w, rowp, colp, norm_w)


def _merge_kernel(ys_ref, yg_ref, gs_ref, gg_ref, x_ref, wus_ref, wug_ref, wo_ref,
                  gm_ref, g1_ref, b1_ref, sc_ref, sh_ref, xo_ref, ho_ref):
    a = _dot(ys_ref[...], wus_ref[...])
    b = _dot(yg_ref[...], wug_ref[...])
    m = _sigmoid(gs_ref[...]) * a + _sigmoid(gg_ref[...]) * b
    y = _dot(m.astype(BF16), wo_ref[...])
    xn = _ln(DEEPNORM_ALPHA * x_ref[...] + gm_ref[...] * y) * g1_ref[...] + b1_ref[...]
    xo_ref[...] = xn
    ho_ref[...] = (_ln(xn) * (1.0 + sc_ref[...]) + sh_ref[...]).astype(BF16)


def _merge_call(layer, ys, yg, p_main, x, w_us, w_ug, w_out, g_m, ln_g, ln_b, sc_f, sh_f):
    l, d = x.shape
    tm = 256
    vec = pl.BlockSpec((1, d), lambda i: (0, 0))
    once = lambda shape: pl.BlockSpec((None,) + shape, lambda i: (layer, 0, 0), pipeline_mode=pl.Buffered(1))
    return pl.pallas_call(
        _merge_kernel,
        grid=(l // tm,),
        in_specs=[pl.BlockSpec((tm, SSM_WIDTH), lambda i: (i, 0)),
                  pl.BlockSpec((tm, GDN_VDIM), lambda i: (i, 0)),
                  pl.BlockSpec((tm, d), lambda i: (i, COL_GATE_S // d)),
                  pl.BlockSpec((tm, d), lambda i: (i, COL_GATE_G // d)),
                  pl.BlockSpec((tm, d), lambda i: (i, 0)),
                  once((SSM_WIDTH, d)), once((GDN_VDIM, d)), once((d, d)),
                  vec, vec, vec, vec, vec],
        out_specs=[pl.BlockSpec((tm, d), lambda i: (i, 0)),
                   pl.BlockSpec((tm, d), lambda i: (i, 0))],
        out_shape=[jax.ShapeDtypeStruct((l, d), F32), jax.ShapeDtypeStruct((l, d), BF16)],
        compiler_params=_params(("arbitrary",)),
        name="merge_out_ln1",
    )(ys, yg, p_main, p_main, x, w_us, w_ug, w_out, g_m, ln_g, ln_b, sc_f, sh_f)


def _ffn_kernel(with_next, h_ref, wg_ref, wu_ref, wo_ref, x_ref, gf_ref, g2_ref, b2_ref, sc_ref, sh_ref,
                xo_ref, *rest):
    ho_ref, acc_ref = rest if with_next else (None, rest[0])
    j = pl.program_id(1)

    @pl.when(j == 0)
    def _():
        acc_ref[...] = jnp.zeros_like(acc_ref)

    h = h_ref[...]
    gate = _dot(h, wg_ref[...])
    up = _dot(h, wu_ref[...])
    acc_ref[...] += _dot((_silu(gate) * up).astype(BF16), wo_ref[...])

    @pl.when(j == pl.num_programs(1) - 1)
    def _():
        xn = _ln(DEEPNORM_ALPHA * x_ref[...] + gf_ref[...] * acc_ref[...]) * g2_ref[...] + b2_ref[...]
        xo_ref[...] = xn
        if with_next:
            ho_ref[...] = (_ln(xn) * (1.0 + sc_ref[...]) + sh_ref[...]).astype(BF16)


def _ffn_call(layer, h, w_in, w_out, x, g_f, ln_g, ln_b, sc_n, sh_n, with_next):
    l, d = x.shape
    f = w_out.shape[1]
    tm, th = 512, 512
    nj = f // th
    vec = pl.BlockSpec((1, d), lambda i, j: (0, 0))
    row_tile = pl.BlockSpec((tm, d), lambda i, j: (i, 0))
    out_specs = [row_tile] + ([row_tile] if with_next else [])
    out_shape = [jax.ShapeDtypeStruct((l, d), F32)] + ([jax.ShapeDtypeStruct((l, d), BF16)] if with_next else [])
    outs = pl.pallas_call(
        functools.partial(_ffn_kernel, with_next),
        grid=(l // tm, nj),
        in_specs=[row_tile,
                  pl.BlockSpec((None, d, th), lambda i, j: (layer, 0, j)),
                  pl.BlockSpec((None, d, th), lambda i, j: (layer, 0, nj + j)),
                  pl.BlockSpec((None, th, d), lambda i, j: (layer, j, 0)),
                  row_tile,
                  vec, vec, vec, vec, vec],
        out_specs=out_specs,
        out_shape=out_shape,
        scratch_shapes=[pltpu.VMEM((tm, d), F32)],
        compiler_params=_params(("arbitrary", "arbitrary")),
        name="ffn_ln2",
    )(h, w_in, w_in, w_out, x, g_f, ln_g, ln_b, sc_n, sh_n)
    return (outs[0], outs[1]) if with_next else (outs[0], None)


def _split_w_in(w):
    o_small = SSM_WIDTH + 2 * GDN_KDIM + 2 * GDN_VDIM
    o_gates = o_small + SMALL_ROWS
    return (w[:, :, o_gates:o_gates + 2 * D_MODEL].astype(BF16), w.astype(BF16),
            w[:, :, o_small:o_small + LANES].astype(BF16),
            jnp.swapaxes(w[:, :, o_small:o_gates], 1, 2).astype(BF16))


def kernel(x, c, w_ada, b_ada, w_in, ssm_lam_re, ssm_lam_im, ssm_log_dt, ssm_b_re, ssm_b_im, ssm_c_re, ssm_c_im, ssm_d, ssm_w_glu, ssm_b_glu, gdn_conv_w, gdn_a_log, gdn_dt_bias, gdn_norm_w, w_up_ssm, w_up_gdn, w_mix_out, ln1_g, ln1_b, ffn_w_in, ffn_w_out, ln2_g, ln2_b):
    bsz, seq, d = x.shape
    assert bsz == 1 and d == D_MODEL and seq % 1024 == 0
    xs = x.reshape(seq, d).astype(F32)
    mod = _ada_call(c.astype(F32), w_ada.astype(F32), b_ada.astype(F32))
    sh_m, sc_m, g_m, sh_f, sc_f, g_f = [mod[:, :, i * d:(i + 1) * d] for i in range(6)]

    w_gates, w_all, w_small, w_small_t = _split_w_in(w_in)
    w_glu_b = ssm_w_glu.astype(BF16)
    w_us_b, w_ug_b, w_out_b = w_up_ssm.astype(BF16), w_up_gdn.astype(BF16), w_mix_out.astype(BF16)
    ffn_in_b, ffn_out_b = ffn_w_in.astype(BF16), ffn_w_out.astype(BF16)

    h = _lnmod_call(xs, sc_m[0], sh_m[0])
    for l in range(DEPTH):
        p_main, p_small, p_small_t = _proj_call(l, h, w_gates, w_all, w_small, w_small_t)

        bmat, cmat, a_tab, aseg_tab = _s5_tables(ssm_lam_re[l], ssm_lam_im[l], ssm_log_dt[l], ssm_b_re[l],
                                                 ssm_b_im[l], ssm_c_re[l], ssm_c_im[l])
        ys = _s5_call(l, p_main, bmat, cmat, a_tab, aseg_tab, ssm_d[l].astype(F32).reshape(1, -1),
                      w_glu_b, ssm_b_glu[l].astype(F32).reshape(1, -1))

        neg_a = -jnp.exp(gdn_a_log[l].astype(F32))
        dtb = gdn_dt_bias[l].astype(F32)
        zeros = jnp.zeros((GDN_HEADS,), F32)
        decay_a = jnp.concatenate([zeros, neg_a])
        decay_b = jnp.concatenate([zeros, dtb])
        rowp = jnp.pad(jnp.stack([decay_a, decay_b]), ((0, 0), (0, LANES - SMALL_ROWS)))
        colp = jnp.stack([decay_a, decay_b], axis=1)
        yg = _gdn_call(p_main, p_small, p_small_t, gdn_conv_w[l].astype(F32), rowp, colp,
                       gdn_norm_w[l].astype(F32).reshape(1, -1))

        xs, h2 = _merge_call(l, ys, yg, p_main, xs, w_us_b, w_ug_b, w_out_b, g_m[l],
                             ln1_g[l].astype(F32).reshape(1, -1), ln1_b[l].astype(F32).reshape(1, -1),
                             sc_f[l], sh_f[l])
        nxt = min(l + 1, DEPTH - 1)
        xs, h = _ffn_call(l, h2, ffn_in_b, ffn_out_b, xs, g_f[l],
                          ln2_g[l].astype(F32).reshape(1, -1), ln2_b[l].astype(F32).reshape(1, -1),
                          sc_m[nxt], sh_m[nxt], with_next=l + 1 < DEPTH)
    return xs.reshape(bsz, seq, d)
```

```python
import functools
import math

import jax
import jax.numpy as jnp
from jax import lax
from jax.experimental import pallas as pl
from jax.experimental.pallas import tpu as pltpu

F32 = jnp.float32
BF16 = jnp.bfloat16

D_MODEL = 2048
DEPTH = 2
CHUNK = 64
SSM_WIDTH = D_MODEL // 2
SSM_GROUP = 16
SSM_GROUPS = SSM_WIDTH // SSM_GROUP
SSM_STATE = 64
GDN_HEADS = 8
GDN_DK = 128
GDN_DV = 128
GDN_KDIM = GDN_HEADS * GDN_DK
GDN_VDIM = GDN_HEADS * GDN_DV
CONV_WIDTH = 4
FFN_HIDDEN = -(-8 * D_MODEL // (3 * 256)) * 256
DEEPNORM_ALPHA = (2 * DEPTH) ** 0.25
LN_EPS = 1e-5
NORM_EPS = 1e-6

SUBLANES = 8
LANES = 128
VMEM_LIMIT_BYTES = 56 * 1024 * 1024

COL_GATE_S = 0
COL_GATE_G = D_MODEL
COL_U = 2 * D_MODEL
COL_Q = COL_U + SSM_WIDTH
COL_K = COL_Q + GDN_KDIM
COL_V = COL_K + GDN_KDIM
COL_Z = COL_V + GDN_VDIM
PROJ_MAIN = COL_Z + GDN_VDIM
SMALL_ROWS = 2 * GDN_HEADS

S5_SEG = 64
S5_BLOCK = SUBLANES * S5_SEG
S5_CLUSTERS = SSM_WIDTH // LANES
S5_CSTATE = (LANES // SSM_GROUP) * SSM_STATE

GDN_BLOCK = 128


def _params(sem):
    return pltpu.CompilerParams(dimension_semantics=sem, vmem_limit_bytes=VMEM_LIMIT_BYTES)


def _ln(x):
    mu = jnp.mean(x, axis=-1, keepdims=True)
    xc = x - mu
    var = jnp.mean(xc * xc, axis=-1, keepdims=True)
    return xc * lax.rsqrt(var + LN_EPS)


def _sigmoid(x):
    return 1.0 / (1.0 + jnp.exp(-x))


def _silu(x):
    return x * _sigmoid(x)


def _softplus(x):
    return jnp.maximum(x, 0.0) + jnp.log(1.0 + jnp.exp(-jnp.abs(x)))


def _dot(a, b):
    return jnp.dot(a, b, preferred_element_type=F32)


def _dot_nt(a, b):
    return lax.dot_general(a, b, (((1,), (1,)), ((), ())), preferred_element_type=F32)


def _ada_kernel(c_ref, w_ref, b_ref, o_ref):
    c = c_ref[...]
    s = _silu(c).astype(BF16)
    o_ref[0] = _dot(s, w_ref[0].astype(BF16)) + b_ref[0]


def _ada_call(c, w_ada, b_ada):
    nl, d, n = w_ada.shape
    tn = 1024
    c8 = jnp.broadcast_to(c, (SUBLANES, d))
    out = pl.pallas_call(
        _ada_kernel,
        grid=(nl, n // tn),
        in_specs=[pl.BlockSpec((SUBLANES, d), lambda l, j: (0, 0)),
                  pl.BlockSpec((1, d, tn), lambda l, j: (l, 0, j)),
                  pl.BlockSpec((1, 1, tn), lambda l, j: (l, 0, j))],
        out_specs=pl.BlockSpec((1, SUBLANES, tn), lambda l, j: (l, 0, j)),
        out_shape=jax.ShapeDtypeStruct((nl, SUBLANES, n), F32),
        compiler_params=_params(("arbitrary", "arbitrary")),
        name="adaln_mod",
    )(c8, w_ada, b_ada.reshape(nl, 1, n))
    return out[:, 0:1, :]


def _lnmod_kernel(x_ref, sc_ref, sh_ref, h_ref):
    y = _ln(x_ref[...])
    h_ref[...] = (y * (1.0 + sc_ref[...]) + sh_ref[...]).astype(BF16)


def _lnmod_call(x, sc, sh):
    l, d = x.shape
    tm = 512
    return pl.pallas_call(
        _lnmod_kernel,
        grid=(l // tm,),
        in_specs=[pl.BlockSpec((tm, d), lambda i: (i, 0)),
                  pl.BlockSpec((1, d), lambda i: (0, 0)),
                  pl.BlockSpec((1, d), lambda i: (0, 0))],
        out_specs=pl.BlockSpec((tm, d), lambda i: (i, 0)),
        out_shape=jax.ShapeDtypeStruct((l, d), BF16),
        compiler_params=_params(("arbitrary",)),
        name="ln0_mod",
    )(x, sc, sh)


PROJ_TN = 1024
PROJ_GATE_TILES = 2 * D_MODEL // PROJ_TN


def _proj_kernel(h_ref, wg_ref, wa_ref, ws_ref, wst_ref, pm_ref, ps_ref, pst_ref):
    j = pl.program_id(1)
    h = h_ref[...]

    @pl.when(j < PROJ_GATE_TILES)
    def _():
        pm_ref[...] = _dot_nt(h, wg_ref[...])

    @pl.when(j >= PROJ_GATE_TILES)
    def _():
        pm_ref[...] = _dot_nt(h, wa_ref[...])

    @pl.when(j == 0)
    def _():
        ps_ref[...] = _dot_nt(h, ws_ref[...])
        pst_ref[...] = _dot_nt(wst_ref[...], h)


def _proj_call(layer, h, wt_gates, wt_all):
    l, d = h.shape
    tm, tn = 1024, PROJ_TN
    ng = PROJ_GATE_TILES
    n = PROJ_MAIN
    o_small = COL_Z + GDN_VDIM - COL_U
    return pl.pallas_call(
        _proj_kernel,
        grid=(l // tm, n // tn),
        in_specs=[pl.BlockSpec((tm, d), lambda i, j: (i, 0)),
                  pl.BlockSpec((None, tn, d), lambda i, j: (layer, jnp.minimum(j, ng - 1), 0)),
                  pl.BlockSpec((None, tn, d), lambda i, j: (layer, jnp.maximum(j - ng, 0), 0)),
                  pl.BlockSpec((None, LANES, d), lambda i, j: (layer, o_small // LANES, 0)),
                  pl.BlockSpec((None, SMALL_ROWS, d), lambda i, j: (layer, o_small // SMALL_ROWS, 0))],
        out_specs=[pl.BlockSpec((tm, tn), lambda i, j: (i, j)),
                   pl.BlockSpec((tm, LANES), lambda i, j: (i, 0)),
                   pl.BlockSpec((SMALL_ROWS, tm), lambda i, j: (0, i))],
        out_shape=[jax.ShapeDtypeStruct((l, n), F32),
                   jax.ShapeDtypeStruct((l, LANES), F32),
                   jax.ShapeDtypeStruct((SMALL_ROWS, l), F32)],
        compiler_params=_params(("arbitrary", "arbitrary")),
        name="in_proj",
    )(h, wt_gates, wt_all, wt_all, wt_all)


def _s5_kernel(*refs):
    u_refs = refs[:S5_CLUSTERS]
    (bmat_ref, cmat_ref, a_ref, aseg_ref, dskip_ref, wglu_ref, bglu_ref, o_ref,
     bu_ref, z_ref, carry_ref, outp_ref) = refs[S5_CLUSTERS:]

    @pl.when(pl.program_id(0) == 0)
    def _():
        carry_ref[...] = jnp.zeros_like(carry_ref)

    sub = lax.broadcasted_iota(jnp.int32, (SUBLANES, S5_CSTATE), 0)
    half = S5_CSTATE
    zero = jnp.zeros((SUBLANES, half), F32)

    def in_proj(cg):
        ucl = jnp.concatenate(
            [u_refs[cg][pl.ds(i, SUBLANES, stride=S5_SEG), :] for i in range(S5_SEG)], axis=0)
        bu_ref[cg % 2] = _dot(ucl.astype(BF16), bmat_ref[cg])
        return ucl

    ucls = {0: in_proj(0)}
    gate = None
    for cg in range(S5_CLUSTERS):
        cols = slice(cg * LANES, (cg + 1) * LANES)
        buf = bu_ref.at[cg % 2]
        if cg + 1 < S5_CLUSTERS:
            ucls[cg + 1] = in_proj(cg + 1)
        ar = jnp.broadcast_to(a_ref[cg, 0:1, :], (SUBLANES, half))
        ai = jnp.broadcast_to(a_ref[cg, 1:2, :], (SUBLANES, half))

        def step(i, xr, xi):
            rows = slice(i * SUBLANES, (i + 1) * SUBLANES)
            bur, bui = buf[rows, 0:half], buf[rows, half:2 * half]
            return ar * xr - ai * xi + bur, ar * xi + ai * xr + bui

        er, ei = zero, zero
        for i in range(S5_SEG):
            er, ei = step(i, er, ei)

        pr = jnp.broadcast_to(aseg_ref[cg, 0:1, :], (SUBLANES, half))
        pi = jnp.broadcast_to(aseg_ref[cg, 1:2, :], (SUBLANES, half))
        cr = carry_ref[cg, :, 0:half]
        ci = carry_ref[cg, :, half:2 * half]
        xr, xi = zero, zero
        for s in range(SUBLANES):
            xr = jnp.where(sub == s, cr, xr)
            xi = jnp.where(sub == s, ci, xi)
            esr = jnp.broadcast_to(er[s:s + 1, :], (SUBLANES, half))
            esi = jnp.broadcast_to(ei[s:s + 1, :], (SUBLANES, half))
            cr, ci = pr * cr - pi * ci + esr, pr * ci + pi * cr + esi
        carry_ref[cg, :, 0:half] = cr
        carry_ref[cg, :, half:2 * half] = ci

        for i in range(S5_SEG):
            xr, xi = step(i, xr, xi)
            rows = slice(i * SUBLANES, (i + 1) * SUBLANES)
            buf[rows, 0:half] = xr
            buf[rows, half:2 * half] = xi

        y = _dot(buf[...].astype(BF16), cmat_ref[cg]) + dskip_ref[:, cols] * ucls.pop(cg)
        z_ref[:, cols] = jax.nn.gelu(y, approximate=True)
        if cg % 2 == 1:
            pair = slice((cg - 1) * LANES, (cg + 1) * LANES)
            part = _dot(z_ref[:, pair].astype(BF16), wglu_ref[pair, :])
            gate = part if gate is None else gate + part

    outp = z_ref[...] * _sigmoid(gate + bglu_ref[...])
    for cg in range(S5_CLUSTERS):
        cols = slice(cg * LANES, (cg + 1) * LANES)
        for i in range(S5_SEG):
            outp_ref[cg, pl.ds(i, SUBLANES, stride=S5_SEG), :] = outp[i * SUBLANES:(i + 1) * SUBLANES, cols]
        o_ref[:, cols] = outp_ref[cg].astype(BF16)


def _s5_call(layer, p_main, bmat, cmat, a_tab, aseg_tab, d_skip, w_glu, b_glu):
    l = p_main.shape[0]
    w = SSM_WIDTH
    full = lambda *shape: pl.BlockSpec(shape, lambda b: (0,) * len(shape))
    return pl.pallas_call(
        _s5_kernel,
        grid=(l // S5_BLOCK,),
        in_specs=[pl.BlockSpec((S5_BLOCK, LANES), functools.partial(lambda cg, b: (b, COL_U // LANES + cg), cg))
                  for cg in range(S5_CLUSTERS)] + [
                  full(S5_CLUSTERS, LANES, 2 * S5_CSTATE),
                  full(S5_CLUSTERS, 2 * S5_CSTATE, LANES),
                  full(S5_CLUSTERS, 2, S5_CSTATE),
                  full(S5_CLUSTERS, 2, S5_CSTATE),
                  full(1, w), pl.BlockSpec((None, w, w), lambda b: (layer, 0, 0)), full(1, w)],
        out_specs=pl.BlockSpec((S5_BLOCK, w), lambda b: (b, 0)),
        out_shape=jax.ShapeDtypeStruct((l, w), BF16),
        scratch_shapes=[pltpu.VMEM((2, S5_BLOCK, 2 * S5_CSTATE), F32),
                        pltpu.VMEM((S5_BLOCK, w), F32),
                        pltpu.VMEM((S5_CLUSTERS, SUBLANES, 2 * S5_CSTATE), F32),
                        pltpu.VMEM((S5_CLUSTERS, S5_BLOCK, LANES), F32)],
        compiler_params=_params(("arbitrary",)),
        name="s5_mixer",
    )(*([p_main] * S5_CLUSTERS), bmat, cmat, a_tab, aseg_tab, d_skip, w_glu, b_glu)


def _s5_tables(lam_re, lam_im, log_dt, b_re, b_im, c_re, c_im):
    lr, li = lam_re.astype(F32), lam_im.astype(F32)
    dt = jnp.exp(log_dt.astype(F32))[:, None]
    mag = jnp.exp(lr * dt)
    ang = li * dt
    ab_re, ab_im = mag * jnp.cos(ang), mag * jnp.sin(ang)
    den = lr * lr + li * li
    nr, ni = ab_re - 1.0, ab_im
    f_re = (nr * lr + ni * li) / den
    f_im = (ni * lr - nr * li) / den
    br, bi = b_re.astype(F32), b_im.astype(F32)
    bb_re = f_re[..., None] * br - f_im[..., None] * bi
    bb_im = f_re[..., None] * bi + f_im[..., None] * br
    gpc = LANES // SSM_GROUP
    eye = jnp.eye(gpc, dtype=F32)

    def in_block(bb):
        t = bb.reshape(S5_CLUSTERS, gpc, SSM_STATE, SSM_GROUP)
        m = jnp.einsum('cgph,gk->cghkp', t, eye)
        return m.reshape(S5_CLUSTERS, gpc * SSM_GROUP, gpc * SSM_STATE)

    def out_block(cc):
        t = cc.reshape(S5_CLUSTERS, gpc, SSM_GROUP, SSM_STATE)
        m = jnp.einsum('cghp,gk->cgpkh', t, eye)
        return m.reshape(S5_CLUSTERS, gpc * SSM_STATE, gpc * SSM_GROUP)

    bmat = jnp.concatenate([in_block(bb_re), in_block(bb_im)], axis=-1).astype(BF16)
    cmat = jnp.concatenate([out_block(c_re.astype(F32)), -out_block(c_im.astype(F32))], axis=1).astype(BF16)

    def tab(re, im):
        return jnp.stack([re.reshape(S5_CLUSTERS, S5_CSTATE), im.reshape(S5_CLUSTERS, S5_CSTATE)], axis=1)

    a_tab = tab(ab_re, ab_im)
    pr, pi = ab_re, ab_im
    for _ in range(int(math.log2(S5_SEG))):
        pr, pi = pr * pr - pi * pi, 2.0 * pr * pi
    return bmat, cmat, a_tab, tab(pr, pi)


def _tri_inverse_batch(ms, row, col):
    blk = 16

    def block_id(idx, size):
        return jnp.right_shift(idx, int(math.log2(size)))

    same = block_id(row, blk) == block_id(col, blk)
    eye = jnp.where(row == col, 1.0, 0.0)
    ps = [jnp.where(same, -m, 0.0) for m in ms]
    xs = [eye + p for p in ps]
    for _ in range(3):
        pbs = [p.astype(BF16) for p in ps]
        ps = [_dot(pb, pb) for pb in pbs]
        xs = [x + _dot(p.astype(BF16), x.astype(BF16)) for p, x in zip(ps, xs)]
    while blk < CHUNK:
        join = jnp.where(block_id(row, 2 * blk) == block_id(col, 2 * blk),
                         jnp.where(block_id(row, blk) != block_id(col, blk), 1.0, 0.0), 0.0)
        cbs = [(m * join).astype(BF16) for m in ms]
        xbs = [x.astype(BF16) for x in xs]
        ts = [_dot(xb, cb).astype(BF16) for xb, cb in zip(xbs, cbs)]
        xs = [x - _dot(t, xb) for x, t, xb in zip(xs, ts, xbs)]
        blk *= 2
    return xs


def _gdn_kernel(q_ref, k_ref, v_ref, z_ref, ps_ref, pst_ref, cw_ref, rowp_ref, colp_ref, nw_ref, o_ref,
                s_ref, tq_ref, tk_ref, tv_ref):
    r = GDN_BLOCK
    nchunks = r // CHUNK
    heads = range(GDN_HEADS)

    @pl.when(pl.program_id(0) == 0)
    def _():
        s_ref[...] = jnp.zeros_like(s_ref)
        tq_ref[...] = jnp.zeros_like(tq_ref)
        tk_ref[...] = jnp.zeros_like(tk_ref)
        tv_ref[...] = jnp.zeros_like(tv_ref)

    def conv_silu(x_ref, tail_ref, col0):
        x = x_ref[...]
        ext = jnp.concatenate([tail_ref[...], x], axis=0)
        w = cw_ref[:, col0:col0 + GDN_KDIM]
        y = w[CONV_WIDTH - 1:CONV_WIDTH, :] * x
        for j in range(1, CONV_WIDTH):
            y = y + w[CONV_WIDTH - 1 - j:CONV_WIDTH - j, :] * ext[SUBLANES - j:SUBLANES - j + r, :]
        tail_ref[...] = x[r - SUBLANES:r, :]
        return _silu(y)

    q = conv_silu(q_ref, tq_ref, 0)
    k = conv_silu(k_ref, tk_ref, GDN_KDIM)
    v = conv_silu(v_ref, tv_ref, 2 * GDN_KDIM)

    ps = ps_ref[...]
    rowi = lax.broadcasted_iota(jnp.int32, (r, LANES), 0)
    g_all = rowp_ref[0:1, :] * _softplus(ps + rowp_ref[1:2, :])
    for sh in (1, 2, 4, 8, 16, 32):
        g_all = g_all + jnp.where((rowi & (CHUNK - 1)) >= sh, pltpu.roll(g_all, sh, 0), 0.0)
    beta_all = _sigmoid(ps)
    pst = pst_ref[...]
    lane_t = lax.broadcasted_iota(jnp.int32, (SMALL_ROWS, r), 1)
    g_t = colp_ref[:, 0:1] * _softplus(pst + colp_ref[:, 1:2])
    for sh in (1, 2, 4, 8, 16, 32):
        g_t = g_t + jnp.where((lane_t & (CHUNK - 1)) >= sh, pltpu.roll(g_t, sh, 1), 0.0)

    row = lax.broadcasted_iota(jnp.int32, (CHUNK, CHUNK), 0)
    col = lax.broadcasted_iota(jnp.int32, (CHUNK, CHUNK), 1)
    causal = row >= col
    strict = row > col
    nw = nw_ref[...]

    qn, kn, vh = [], [], []
    for h in heads:
        hl = slice(h * GDN_DK, (h + 1) * GDN_DK)
        qh, kh = q[:, hl], k[:, hl]
        qn.append(qh * lax.rsqrt(jnp.sum(qh * qh, axis=-1, keepdims=True) + NORM_EPS) * (GDN_DK ** -0.5))
        kn.append(kh * lax.rsqrt(jnp.sum(kh * kh, axis=-1, keepdims=True) + NORM_EPS))
        vh.append(v[:, hl])

    items = [(c, h) for c in range(nchunks) for h in heads]
    it = {}
    for (c, h) in items:
        rows = slice(c * CHUNK, (c + 1) * CHUNK)
        d = {}
        d['q'], d['k'], d['v'] = qn[h][rows], kn[h][rows], vh[h][rows]
        d['beta'] = beta_all[rows, h:h + 1]
        gcc = g_all[rows, GDN_HEADS + h:GDN_HEADS + h + 1]
        gcr = g_t[GDN_HEADS + h:GDN_HEADS + h + 1, c * CHUNK:(c + 1) * CHUNK]
        d['g_end'] = gcc[CHUNK - 1:CHUNK, :]
        d['eg'] = jnp.exp(gcc)
        d['etail'] = jnp.exp(d['g_end'] - gcc)
        d['decay'] = jnp.where(causal, jnp.exp(jnp.where(causal, gcc - gcr, 0.0)), 0.0)
        d['kb'] = d['k'] * d['beta']
        it[(c, h)] = d

    kqs = [_dot_nt(jnp.concatenate([it[i]['kb'], it[i]['q']], axis=0).astype(BF16), it[i]['k'].astype(BF16))
           for i in items]
    ms = [jnp.where(strict, kq[:CHUNK] * it[i]['decay'], 0.0) for kq, i in zip(kqs, items)]
    attns = [(kq[CHUNK:] * it[i]['decay']).astype(BF16) for kq, i in zip(kqs, items)]
    t_invs = _tri_inverse_batch(ms, row, col)
    uws = [_dot(t.astype(BF16),
                jnp.concatenate([it[i]['v'] * it[i]['beta'], it[i]['kb'] * it[i]['eg']], axis=1).astype(BF16))
           for t, i in zip(t_invs, items)]
    for i, uw, attn in zip(items, uws, attns):
        d = it[i]
        d['u'] = uw[:, :GDN_DV]
        d['wqd'] = jnp.concatenate([uw[:, GDN_DV:], d['q'] * d['eg']], axis=0).astype(BF16)
        d['kt'] = (d['k'] * d['etail']).astype(BF16)
        d['attn'] = attn

    states = [s_ref[h] for h in heads]
    for c in range(nchunks):
        rows = slice(c * CHUNK, (c + 1) * CHUNK)
        wqs = [_dot(it[(c, h)]['wqd'], states[h].astype(BF16)) for h in heads]
        v_news = [it[(c, h)]['u'] - wqs[h][:CHUNK] for h in heads]
        v_bfs = [vn.astype(BF16) for vn in v_news]
        o_cs = [wqs[h][CHUNK:] + _dot(it[(c, h)]['attn'], v_bfs[h]) for h in heads]
        states = [states[h] * jnp.exp(it[(c, h)]['g_end']) + lax.dot_general(
            it[(c, h)]['kt'], v_bfs[h], (((0,), (0,)), ((), ())), preferred_element_type=F32) for h in heads]
        for h in heads:
            hl = slice(h * GDN_DV, (h + 1) * GDN_DV)
            o_c = o_cs[h]
            o_n = o_c * lax.rsqrt(jnp.mean(o_c * o_c, axis=-1, keepdims=True) + NORM_EPS) * nw
            o_ref[rows, hl] = (o_n * _silu(z_ref[rows, hl])).astype(BF16)
    for h in heads:
        s_ref[h] = states[h]


def _gdn_call(p_main, p_small, p_small_t, conv_w, rowp, colp, norm_w):
    l = p_main.shape[0]
    r = GDN_BLOCK
    wide = lambda base: pl.BlockSpec((r, GDN_KDIM), lambda b: (b, base // GDN_KDIM))
    full = lambda *shape: pl.BlockSpec(shape, lambda b: (0,) * len(shape))
    return pl.pallas_call(
        _gdn_kernel,
        grid=(l // r,),
        in_specs=[wide(COL_Q), wide(COL_K), wide(COL_V), wide(COL_Z),
                  pl.BlockSpec((r, LANES), lambda b: (b, 0)),
                  pl.BlockSpec((SMALL_ROWS, r), lambda b: (0, b)),
                  full(CONV_WIDTH, 2 * GDN_KDIM + GDN_VDIM),
                  full(2, LANES), full(SMALL_ROWS, 2), full(1, GDN_DV)],
        out_specs=pl.BlockSpec((r, GDN_VDIM), lambda b: (b, 0)),
        out_shape=jax.ShapeDtypeStruct((l, GDN_VDIM), BF16),
        scratch_shapes=[pltpu.VMEM((GDN_HEADS, GDN_DK, GDN_DV), F32),
                        pltpu.VMEM((SUBLANES, GDN_KDIM), F32),
                        pltpu.VMEM((SUBLANES, GDN_KDIM), F32),
                        pltpu.VMEM((SUBLANES, GDN_VDIM), F32)],
        compiler_params=_params(("arbitrary",)),
        name="gated_deltanet",
    )(p_main, p_main, p_main, p_main, p_small, p_small_t, conv_w, rowp, colp, norm_w)


def _merge_kernel(ys_ref, yg_ref, gs_ref, gg_ref, x_ref, wus_ref, wug_ref, wo_ref,
                  gm_ref, g1_ref, b1_ref, sc_ref, sh_ref, xo_ref, ho_ref):
    a = _dot(ys_ref[...], wus_ref[...])
    b = _dot(yg_ref[...], wug_ref[...])
    m = _sigmoid(gs_ref[...]) * a + _sigmoid(gg_ref[...]) * b
    y = _dot(m.astype(BF16), wo_ref[...])
    xn = _ln(DEEPNORM_ALPHA * x_ref[...] + gm_ref[...] * y) * g1_ref[...] + b1_ref[...]
    xo_ref[...] = xn
    ho_ref[...] = (_ln(xn) * (1.0 + sc_ref[...]) + sh_ref[...]).astype(BF16)


def _merge_call(layer, ys, yg, p_main, x, w_us, w_ug, w_out, g_m, ln_g, ln_b, sc_f, sh_f):
    l, d = x.shape
    tm = 256
    vec = pl.BlockSpec((1, d), lambda i: (0, 0))
    once = lambda shape: pl.BlockSpec((None,) + shape, lambda i: (layer, 0, 0), pipeline_mode=pl.Buffered(1))
    return pl.pallas_call(
        _merge_kernel,
        grid=(l // tm,),
        in_specs=[pl.BlockSpec((tm, SSM_WIDTH), lambda i: (i, 0)),
                  pl.BlockSpec((tm, GDN_VDIM), lambda i: (i, 0)),
                  pl.BlockSpec((tm, d), lambda i: (i, COL_GATE_S // d)),
                  pl.BlockSpec((tm, d), lambda i: (i, COL_GATE_G // d)),
                  pl.BlockSpec((tm, d), lambda i: (i, 0)),
                  once((SSM_WIDTH, d)), once((GDN_VDIM, d)), once((d, d)),
                  vec, vec, vec, vec, vec],
        out_specs=[pl.BlockSpec((tm, d), lambda i: (i, 0)),
                   pl.BlockSpec((tm, d), lambda i: (i, 0))],
        out_shape=[jax.ShapeDtypeStruct((l, d), F32), jax.ShapeDtypeStruct((l, d), BF16)],
        compiler_params=_params(("arbitrary",)),
        name="merge_out_ln1",
    )(ys, yg, p_main, p_main, x, w_us, w_ug, w_out, g_m, ln_g, ln_b, sc_f, sh_f)


def _ffn_kernel(with_next, h_ref, wg_ref, wu_ref, wo_ref, x_ref, gf_ref, g2_ref, b2_ref, sc_ref, sh_ref,
                xo_ref, *rest):
    ho_ref, acc_ref = rest if with_next else (None, rest[0])
    j = pl.program_id(1)

    @pl.when(j == 0)
    def _():
        acc_ref[...] = jnp.zeros_like(acc_ref)

    h = h_ref[...]
    gate = _dot(h, wg_ref[...])
    up = _dot(h, wu_ref[...])
    acc_ref[...] += _dot((_silu(gate) * up).astype(BF16), wo_ref[...])

    @pl.when(j == pl.num_programs(1) - 1)
    def _():
        xn = _ln(DEEPNORM_ALPHA * x_ref[...] + gf_ref[...] * acc_ref[...]) * g2_ref[...] + b2_ref[...]
        xo_ref[...] = xn
        if with_next:
            ho_ref[...] = (_ln(xn) * (1.0 + sc_ref[...]) + sh_ref[...]).astype(BF16)


def _ffn_call(layer, h, w_in, w_out, x, g_f, ln_g, ln_b, sc_n, sh_n, with_next):
    l, d = x.shape
    f = w_out.shape[1]
    tm, th = 512, 512
    nj = f // th
    vec = pl.BlockSpec((1, d), lambda i, j: (0, 0))
    row_tile = pl.BlockSpec((tm, d), lambda i, j: (i, 0))
    out_specs = [row_tile] + ([row_tile] if with_next else [])
    out_shape = [jax.ShapeDtypeStruct((l, d), F32)] + ([jax.ShapeDtypeStruct((l, d), BF16)] if with_next else [])
    outs = pl.pallas_call(
        functools.partial(_ffn_kernel, with_next),
        grid=(l // tm, nj),
        in_specs=[row_tile,
                  pl.BlockSpec((None, d, th), lambda i, j: (layer, 0, j)),
                  pl.BlockSpec((None, d, th), lambda i, j: (layer, 0, nj + j)),
                  pl.BlockSpec((None, th, d), lambda i, j: (layer, j, 0)),
                  row_tile,
                  vec, vec, vec, vec, vec],
        out_specs=out_specs,
        out_shape=out_shape,
        scratch_shapes=[pltpu.VMEM((tm, d), F32)],
        compiler_params=_params(("arbitrary", "arbitrary")),
        name="ffn_ln2",
    )(h, w_in, w_in, w_out, x, g_f, ln_g, ln_b, sc_n, sh_n)
    return (outs[0], outs[1]) if with_next else (outs[0], None)


def _split_w_in(w):
    o_gates = SSM_WIDTH + 2 * GDN_KDIM + 2 * GDN_VDIM + SMALL_ROWS
    wt = jnp.swapaxes(w.astype(BF16), 1, 2)
    return wt[:, o_gates:o_gates + 2 * D_MODEL, :], wt


def kernel(x, c, w_ada, b_ada, w_in, ssm_lam_re, ssm_lam_im, ssm_log_dt, ssm_b_re, ssm_b_im, ssm_c_re, ssm_c_im, ssm_d, ssm_w_glu, ssm_b_glu, gdn_conv_w, gdn_a_log, gdn_dt_bias, gdn_norm_w, w_up_ssm, w_up_gdn, w_mix_out, ln1_g, ln1_b, ffn_w_in, ffn_w_out, ln2_g, ln2_b):
    bsz, seq, d = x.shape
    assert bsz == 1 and d == D_MODEL and seq % 1024 == 0
    xs = x.reshape(seq, d).astype(F32)
    mod = _ada_call(c.astype(F32), w_ada.astype(F32), b_ada.astype(F32))
    sh_m, sc_m, g_m, sh_f, sc_f, g_f = [mod[:, :, i * d:(i + 1) * d] for i in range(6)]

    wt_gates, wt_all = _split_w_in(w_in)
    w_glu_b = ssm_w_glu.astype(BF16)
    w_us_b, w_ug_b, w_out_b = w_up_ssm.astype(BF16), w_up_gdn.astype(BF16), w_mix_out.astype(BF16)
    ffn_in_b, ffn_out_b = ffn_w_in.astype(BF16), ffn_w_out.astype(BF16)

    h = _lnmod_call(xs, sc_m[0], sh_m[0])
    for l in range(DEPTH):
        p_main, p_small, p_small_t = _proj_call(l, h, wt_gates, wt_all)

        bmat, cmat, a_tab, aseg_tab = _s5_tables(ssm_lam_re[l], ssm_lam_im[l], ssm_log_dt[l], ssm_b_re[l],
                                                 ssm_b_im[l], ssm_c_re[l], ssm_c_im[l])
        ys = _s5_call(l, p_main, bmat, cmat, a_tab, aseg_tab, ssm_d[l].astype(F32).reshape(1, -1),
                      w_glu_b, ssm_b_glu[l].astype(F32).reshape(1, -1))

        neg_a = -jnp.exp(gdn_a_log[l].astype(F32))
        dtb = gdn_dt_bias[l].astype(F32)
        zeros = jnp.zeros((GDN_HEADS,), F32)
        decay_a = jnp.concatenate([zeros, neg_a])
        decay_b = jnp.concatenate([zeros, dtb])
        rowp = jnp.pad(jnp.stack([decay_a, decay_b]), ((0, 0), (0, LANES - SMALL_ROWS)))
        colp = jnp.stack([decay_a, decay_b], axis=1)
        yg = _gdn_call(p_main, p_small, p_small_t, gdn_conv_w[l].astype(F32), rowp, colp,
                       gdn_norm_w[l].astype(F32).reshape(1, -1))

        xs, h2 = _merge_call(l, ys, yg, p_main, xs, w_us_b, w_ug_b, w_out_b, g_m[l],
                             ln1_g[l].astype(F32).reshape(1, -1), ln1_b[l].astype(F32).reshape(1, -1),
                             sc_f[l], sh_f[l])
        nxt = min(l + 1, DEPTH - 1)
        xs, h = _ffn_call(l, h2, ffn_in_b, ffn_out_b, xs, g_f[l],
                          ln2_g[l].astype(F32).reshape(1, -1), ln2_b[l].astype(F32).reshape(1, -1),
                          sc_m[nxt], sh_m[nxt], with_next=l + 1 < DEPTH)
    return xs.reshape(bsz, seq, d)
```

```python
import functools
import math

import jax
import jax.numpy as jnp
from jax import lax
from jax.experimental import pallas as pl
from jax.experimental.pallas import tpu as pltpu

F32 = jnp.float32
BF16 = jnp.bfloat16

D_MODEL = 2048
DEPTH = 2
CHUNK = 64
SSM_WIDTH = D_MODEL // 2
SSM_GROUP = 16
SSM_GROUPS = SSM_WIDTH // SSM_GROUP
SSM_STATE = 64
GDN_HEADS = 8
GDN_DK = 128
GDN_DV = 128
GDN_KDIM = GDN_HEADS * GDN_DK
GDN_VDIM = GDN_HEADS * GDN_DV
CONV_WIDTH = 4
FFN_HIDDEN = -(-8 * D_MODEL // (3 * 256)) * 256
DEEPNORM_ALPHA = (2 * DEPTH) ** 0.25
LN_EPS = 1e-5
NORM_EPS = 1e-6

SUBLANES = 8
LANES = 128
VMEM_LIMIT_BYTES = 56 * 1024 * 1024

COL_GATE_S = 0
COL_GATE_G = D_MODEL
COL_U = 2 * D_MODEL
COL_Q = COL_U + SSM_WIDTH
COL_K = COL_Q + GDN_KDIM
COL_V = COL_K + GDN_KDIM
COL_Z = COL_V + GDN_VDIM
PROJ_MAIN = COL_Z + GDN_VDIM
SMALL_ROWS = 2 * GDN_HEADS

S5_SEG = 64
S5_BLOCK = SUBLANES * S5_SEG
S5_CLUSTERS = SSM_WIDTH // LANES
S5_CSTATE = (LANES // SSM_GROUP) * SSM_STATE

GDN_BLOCK = 256


def _params(sem):
    return pltpu.CompilerParams(dimension_semantics=sem, vmem_limit_bytes=VMEM_LIMIT_BYTES)


def _ln(x):
    mu = jnp.mean(x, axis=-1, keepdims=True)
    xc = x - mu
    var = jnp.mean(xc * xc, axis=-1, keepdims=True)
    return xc * lax.rsqrt(var + LN_EPS)


def _sigmoid(x):
    return 1.0 / (1.0 + jnp.exp(-x))


def _silu(x):
    return x * _sigmoid(x)


def _softplus(x):
    return jnp.maximum(x, 0.0) + jnp.log(1.0 + jnp.exp(-jnp.abs(x)))


def _dot(a, b):
    return jnp.dot(a, b, preferred_element_type=F32)


def _dot_nt(a, b):
    return lax.dot_general(a, b, (((1,), (1,)), ((), ())), preferred_element_type=F32)


def _ada_kernel(c_ref, w_ref, b_ref, o_ref):
    c = c_ref[...]
    s = _silu(c).astype(BF16)
    o_ref[0] = _dot(s, w_ref[0].astype(BF16)) + b_ref[0]


def _ada_call(c, w_ada, b_ada):
    nl, d, n = w_ada.shape
    tn = 1024
    c8 = jnp.broadcast_to(c, (SUBLANES, d))
    out = pl.pallas_call(
        _ada_kernel,
        grid=(nl, n // tn),
        in_specs=[pl.BlockSpec((SUBLANES, d), lambda l, j: (0, 0)),
                  pl.BlockSpec((1, d, tn), lambda l, j: (l, 0, j)),
                  pl.BlockSpec((1, 1, tn), lambda l, j: (l, 0, j))],
        out_specs=pl.BlockSpec((1, SUBLANES, tn), lambda l, j: (l, 0, j)),
        out_shape=jax.ShapeDtypeStruct((nl, SUBLANES, n), F32),
        compiler_params=_params(("arbitrary", "arbitrary")),
        name="adaln_mod",
    )(c8, w_ada, b_ada.reshape(nl, 1, n))
    return out[:, 0:1, :]


def _lnmod_kernel(x_ref, sc_ref, sh_ref, h_ref):
    y = _ln(x_ref[...])
    h_ref[...] = (y * (1.0 + sc_ref[...]) + sh_ref[...]).astype(BF16)


def _lnmod_call(x, sc, sh):
    l, d = x.shape
    tm = 512
    return pl.pallas_call(
        _lnmod_kernel,
        grid=(l // tm,),
        in_specs=[pl.BlockSpec((tm, d), lambda i: (i, 0)),
                  pl.BlockSpec((1, d), lambda i: (0, 0)),
                  pl.BlockSpec((1, d), lambda i: (0, 0))],
        out_specs=pl.BlockSpec((tm, d), lambda i: (i, 0)),
        out_shape=jax.ShapeDtypeStruct((l, d), BF16),
        compiler_params=_params(("arbitrary",)),
        name="ln0_mod",
    )(x, sc, sh)


PROJ_TN = 1024
PROJ_GATE_TILES = 2 * D_MODEL // PROJ_TN


def _proj_kernel(h_ref, wg_ref, wa_ref, ws_ref, wst_ref, pm_ref, ps_ref, pst_ref):
    j = pl.program_id(1)
    h = h_ref[...]

    @pl.when(j < PROJ_GATE_TILES)
    def _():
        pm_ref[...] = _dot_nt(h, wg_ref[...])

    @pl.when(j >= PROJ_GATE_TILES)
    def _():
        pm_ref[...] = _dot_nt(h, wa_ref[...])

    @pl.when(j == 0)
    def _():
        ps_ref[...] = _dot_nt(h, ws_ref[...])
        pst_ref[...] = _dot_nt(wst_ref[...], h)


def _proj_call(layer, h, wt_gates, wt_all):
    l, d = h.shape
    tm, tn = 1024, PROJ_TN
    ng = PROJ_GATE_TILES
    n = PROJ_MAIN
    o_small = COL_Z + GDN_VDIM - COL_U
    return pl.pallas_call(
        _proj_kernel,
        grid=(l // tm, n // tn),
        in_specs=[pl.BlockSpec((tm, d), lambda i, j: (i, 0)),
                  pl.BlockSpec((None, tn, d), lambda i, j: (layer, jnp.minimum(j, ng - 1), 0)),
                  pl.BlockSpec((None, tn, d), lambda i, j: (layer, jnp.maximum(j - ng, 0), 0)),
                  pl.BlockSpec((None, LANES, d), lambda i, j: (layer, o_small // LANES, 0)),
                  pl.BlockSpec((None, SMALL_ROWS, d), lambda i, j: (layer, o_small // SMALL_ROWS, 0))],
        out_specs=[pl.BlockSpec((tm, tn), lambda i, j: (i, j)),
                   pl.BlockSpec((tm, LANES), lambda i, j: (i, 0)),
                   pl.BlockSpec((SMALL_ROWS, tm), lambda i, j: (0, i))],
        out_shape=[jax.ShapeDtypeStruct((l, n), F32),
                   jax.ShapeDtypeStruct((l, LANES), F32),
                   jax.ShapeDtypeStruct((SMALL_ROWS, l), F32)],
        compiler_params=_params(("arbitrary", "arbitrary")),
        name="in_proj",
    )(h, wt_gates, wt_all, wt_all, wt_all)


def _s5_kernel(*refs):
    u_refs = refs[:S5_CLUSTERS]
    (bmat_ref, cmat_ref, a_ref, aseg_ref, dskip_ref, wglu_ref, bglu_ref, o_ref,
     bu_ref, z_ref, carry_ref, outp_ref) = refs[S5_CLUSTERS:]

    @pl.when(pl.program_id(0) == 0)
    def _():
        carry_ref[...] = jnp.zeros_like(carry_ref)

    sub = lax.broadcasted_iota(jnp.int32, (SUBLANES, S5_CSTATE), 0)
    half = S5_CSTATE
    zero = jnp.zeros((SUBLANES, half), F32)

    def in_proj(cg):
        ucl = jnp.concatenate(
            [u_refs[cg][pl.ds(i, SUBLANES, stride=S5_SEG), :] for i in range(S5_SEG)], axis=0)
        bu_ref[cg % 2] = _dot(ucl.astype(BF16), bmat_ref[cg])
        return ucl

    ucls = {0: in_proj(0)}
    gate = None
    for cg in range(S5_CLUSTERS):
        cols = slice(cg * LANES, (cg + 1) * LANES)
        buf = bu_ref.at[cg % 2]
        if cg + 1 < S5_CLUSTERS:
            ucls[cg + 1] = in_proj(cg + 1)
        ar = jnp.broadcast_to(a_ref[cg, 0:1, :], (SUBLANES, half))
        ai = jnp.broadcast_to(a_ref[cg, 1:2, :], (SUBLANES, half))

        def step(i, xr, xi):
            rows = slice(i * SUBLANES, (i + 1) * SUBLANES)
            bur, bui = buf[rows, 0:half], buf[rows, half:2 * half]
            return ar * xr - ai * xi + bur, ar * xi + ai * xr + bui

        er, ei = zero, zero
        for i in range(S5_SEG):
            er, ei = step(i, er, ei)

        pr = jnp.broadcast_to(aseg_ref[cg, 0:1, :], (SUBLANES, half))
        pi = jnp.broadcast_to(aseg_ref[cg, 1:2, :], (SUBLANES, half))
        cr = carry_ref[cg, :, 0:half]
        ci = carry_ref[cg, :, half:2 * half]
        xr, xi = zero, zero
        for s in range(SUBLANES):
            xr = jnp.where(sub == s, cr, xr)
            xi = jnp.where(sub == s, ci, xi)
            esr = jnp.broadcast_to(er[s:s + 1, :], (SUBLANES, half))
            esi = jnp.broadcast_to(ei[s:s + 1, :], (SUBLANES, half))
            cr, ci = pr * cr - pi * ci + esr, pr * ci + pi * cr + esi
        carry_ref[cg, :, 0:half] = cr
        carry_ref[cg, :, half:2 * half] = ci

        for i in range(S5_SEG):
            xr, xi = step(i, xr, xi)
            rows = slice(i * SUBLANES, (i + 1) * SUBLANES)
            buf[rows, 0:half] = xr
            buf[rows, half:2 * half] = xi

        y = _dot(buf[...].astype(BF16), cmat_ref[cg]) + dskip_ref[:, cols] * ucls.pop(cg)
        z_ref[:, cols] = jax.nn.gelu(y, approximate=True)
        if cg % 2 == 1:
            pair = slice((cg - 1) * LANES, (cg + 1) * LANES)
            part = _dot(z_ref[:, pair].astype(BF16), wglu_ref[pair, :])
            gate = part if gate is None else gate + part

    outp = z_ref[...] * _sigmoid(gate + bglu_ref[...])
    for cg in range(S5_CLUSTERS):
        cols = slice(cg * LANES, (cg + 1) * LANES)
        for i in range(S5_SEG):
            outp_ref[cg, pl.ds(i, SUBLANES, stride=S5_SEG), :] = outp[i * SUBLANES:(i + 1) * SUBLANES, cols]
        o_ref[:, cols] = outp_ref[cg].astype(BF16)


def _s5_call(layer, p_main, bmat, cmat, a_tab, aseg_tab, d_skip, w_glu, b_glu):
    l = p_main.shape[0]
    w = SSM_WIDTH
    full = lambda *shape: pl.BlockSpec(shape, lambda b: (0,) * len(shape))
    return pl.pallas_call(
        _s5_kernel,
        grid=(l // S5_BLOCK,),
        in_specs=[pl.BlockSpec((S5_BLOCK, LANES), functools.partial(lambda cg, b: (b, COL_U // LANES + cg), cg))
                  for cg in range(S5_CLUSTERS)] + [
                  full(S5_CLUSTERS, LANES, 2 * S5_CSTATE),
                  full(S5_CLUSTERS, 2 * S5_CSTATE, LANES),
                  full(S5_CLUSTERS, 2, S5_CSTATE),
                  full(S5_CLUSTERS, 2, S5_CSTATE),
                  full(1, w), pl.BlockSpec((None, w, w), lambda b: (layer, 0, 0)), full(1, w)],
        out_specs=pl.BlockSpec((S5_BLOCK, w), lambda b: (b, 0)),
        out_shape=jax.ShapeDtypeStruct((l, w), BF16),
        scratch_shapes=[pltpu.VMEM((2, S5_BLOCK, 2 * S5_CSTATE), F32),
                        pltpu.VMEM((S5_BLOCK, w), F32),
                        pltpu.VMEM((S5_CLUSTERS, SUBLANES, 2 * S5_CSTATE), F32),
                        pltpu.VMEM((S5_CLUSTERS, S5_BLOCK, LANES), F32)],
        compiler_params=_params(("arbitrary",)),
        name="s5_mixer",
    )(*([p_main] * S5_CLUSTERS), bmat, cmat, a_tab, aseg_tab, d_skip, w_glu, b_glu)


def _s5_tables(lam_re, lam_im, log_dt, b_re, b_im, c_re, c_im):
    lr, li = lam_re.astype(F32), lam_im.astype(F32)
    dt = jnp.exp(log_dt.astype(F32))[:, None]
    mag = jnp.exp(lr * dt)
    ang = li * dt
    ab_re, ab_im = mag * jnp.cos(ang), mag * jnp.sin(ang)
    den = lr * lr + li * li
    nr, ni = ab_re - 1.0, ab_im
    f_re = (nr * lr + ni * li) / den
    f_im = (ni * lr - nr * li) / den
    br, bi = b_re.astype(F32), b_im.astype(F32)
    bb_re = f_re[..., None] * br - f_im[..., None] * bi
    bb_im = f_re[..., None] * bi + f_im[..., None] * br
    gpc = LANES // SSM_GROUP
    eye = jnp.eye(gpc, dtype=F32)

    def in_block(bb):
        t = bb.reshape(S5_CLUSTERS, gpc, SSM_STATE, SSM_GROUP)
        m = jnp.einsum('cgph,gk->cghkp', t, eye)
        return m.reshape(S5_CLUSTERS, gpc * SSM_GROUP, gpc * SSM_STATE)

    def out_block(cc):
        t = cc.reshape(S5_CLUSTERS, gpc, SSM_GROUP, SSM_STATE)
        m = jnp.einsum('cghp,gk->cgpkh', t, eye)
        return m.reshape(S5_CLUSTERS, gpc * SSM_STATE, gpc * SSM_GROUP)

    bmat = jnp.concatenate([in_block(bb_re), in_block(bb_im)], axis=-1).astype(BF16)
    cmat = jnp.concatenate([out_block(c_re.astype(F32)), -out_block(c_im.astype(F32))], axis=1).astype(BF16)

    def tab(re, im):
        return jnp.stack([re.reshape(S5_CLUSTERS, S5_CSTATE), im.reshape(S5_CLUSTERS, S5_CSTATE)], axis=1)

    a_tab = tab(ab_re, ab_im)
    pr, pi = ab_re, ab_im
    for _ in range(int(math.log2(S5_SEG))):
        pr, pi = pr * pr - pi * pi, 2.0 * pr * pi
    return bmat, cmat, a_tab, tab(pr, pi)


def _tri_inverse_batch(ms, row, col):
    blk = 16

    def block_id(idx, size):
        return jnp.right_shift(idx, int(math.log2(size)))

    same = block_id(row, blk) == block_id(col, blk)
    eye = jnp.where(row == col, 1.0, 0.0)
    ps = [jnp.where(same, -m, 0.0) for m in ms]
    xs = [eye + p for p in ps]
    for _ in range(3):
        pbs = [p.astype(BF16) for p in ps]
        ps = [_dot(pb, pb) for pb in pbs]
        xs = [x + _dot(p.astype(BF16), x.astype(BF16)) for p, x in zip(ps, xs)]
    while blk < CHUNK:
        join = jnp.where(block_id(row, 2 * blk) == block_id(col, 2 * blk),
                         jnp.where(block_id(row, blk) != block_id(col, blk), 1.0, 0.0), 0.0)
        cbs = [(m * join).astype(BF16) for m in ms]
        xbs = [x.astype(BF16) for x in xs]
        ts = [_dot(xb, cb).astype(BF16) for xb, cb in zip(xbs, cbs)]
        xs = [x - _dot(t, xb) for x, t, xb in zip(xs, ts, xbs)]
        blk *= 2
    return xs


def _gdn_kernel(q_ref, k_ref, v_ref, z_ref, ps_ref, pst_ref, cw_ref, rowp_ref, colp_ref, nw_ref, o_ref,
                s_ref, tq_ref, tk_ref, tv_ref):
    r = GDN_BLOCK
    nchunks = r // CHUNK
    heads = range(GDN_HEADS)

    @pl.when(pl.program_id(0) == 0)
    def _():
        s_ref[...] = jnp.zeros_like(s_ref)
        tq_ref[...] = jnp.zeros_like(tq_ref)
        tk_ref[...] = jnp.zeros_like(tk_ref)
        tv_ref[...] = jnp.zeros_like(tv_ref)

    def conv_silu(x_ref, tail_ref, col0):
        x = x_ref[...]
        ext = jnp.concatenate([tail_ref[...], x], axis=0)
        w = cw_ref[:, col0:col0 + GDN_KDIM]
        y = w[CONV_WIDTH - 1:CONV_WIDTH, :] * x
        for j in range(1, CONV_WIDTH):
            y = y + w[CONV_WIDTH - 1 - j:CONV_WIDTH - j, :] * ext[SUBLANES - j:SUBLANES - j + r, :]
        tail_ref[...] = x[r - SUBLANES:r, :]
        return _silu(y)

    q = conv_silu(q_ref, tq_ref, 0)
    k = conv_silu(k_ref, tk_ref, GDN_KDIM)
    v = conv_silu(v_ref, tv_ref, 2 * GDN_KDIM)

    ps = ps_ref[...]
    rowi = lax.broadcasted_iota(jnp.int32, (r, LANES), 0)
    g_all = rowp_ref[0:1, :] * _softplus(ps + rowp_ref[1:2, :])
    for sh in (1, 2, 4, 8, 16, 32):
        g_all = g_all + jnp.where((rowi & (CHUNK - 1)) >= sh, pltpu.roll(g_all, sh, 0), 0.0)
    beta_all = _sigmoid(ps)
    pst = pst_ref[...]
    lane_t = lax.broadcasted_iota(jnp.int32, (SMALL_ROWS, r), 1)
    g_t = colp_ref[:, 0:1] * _softplus(pst + colp_ref[:, 1:2])
    for sh in (1, 2, 4, 8, 16, 32):
        g_t = g_t + jnp.where((lane_t & (CHUNK - 1)) >= sh, pltpu.roll(g_t, sh, 1), 0.0)

    row = lax.broadcasted_iota(jnp.int32, (CHUNK, CHUNK), 0)
    col = lax.broadcasted_iota(jnp.int32, (CHUNK, CHUNK), 1)
    causal = row >= col
    strict = row > col
    nw = nw_ref[...]

    qn, kn, vh = [], [], []
    for h in heads:
        hl = slice(h * GDN_DK, (h + 1) * GDN_DK)
        qh, kh = q[:, hl], k[:, hl]
        qn.append(qh * lax.rsqrt(jnp.sum(qh * qh, axis=-1, keepdims=True) + NORM_EPS) * (GDN_DK ** -0.5))
        kn.append(kh * lax.rsqrt(jnp.sum(kh * kh, axis=-1, keepdims=True) + NORM_EPS))
        vh.append(v[:, hl])

    items = [(c, h) for c in range(nchunks) for h in heads]
    it = {}
    for (c, h) in items:
        rows = slice(c * CHUNK, (c + 1) * CHUNK)
        d = {}
        d['q'], d['k'], d['v'] = qn[h][rows], kn[h][rows], vh[h][rows]
        d['beta'] = beta_all[rows, h:h + 1]
        gcc = g_all[rows, GDN_HEADS + h:GDN_HEADS + h + 1]
        gcr = g_t[GDN_HEADS + h:GDN_HEADS + h + 1, c * CHUNK:(c + 1) * CHUNK]
        d['g_end'] = gcc[CHUNK - 1:CHUNK, :]
        d['eg'] = jnp.exp(gcc)
        d['etail'] = jnp.exp(d['g_end'] - gcc)
        d['decay'] = jnp.where(causal, jnp.exp(jnp.where(causal, gcc - gcr, 0.0)), 0.0)
        d['kb'] = d['k'] * d['beta']
        it[(c, h)] = d

    kqs = [_dot_nt(jnp.concatenate([it[i]['kb'], it[i]['q']], axis=0).astype(BF16), it[i]['k'].astype(BF16))
           for i in items]
    ms = [jnp.where(strict, kq[:CHUNK] * it[i]['decay'], 0.0) for kq, i in zip(kqs, items)]
    attns = [(kq[CHUNK:] * it[i]['decay']).astype(BF16) for kq, i in zip(kqs, items)]
    t_invs = _tri_inverse_batch(ms, row, col)
    uws = [_dot(t.astype(BF16),
                jnp.concatenate([it[i]['v'] * it[i]['beta'], it[i]['kb'] * it[i]['eg']], axis=1).astype(BF16))
           for t, i in zip(t_invs, items)]
    for i, uw, attn in zip(items, uws, attns):
        d = it[i]
        d['u'] = uw[:, :GDN_DV]
        d['wqd'] = jnp.concatenate([uw[:, GDN_DV:], d['q'] * d['eg']], axis=0).astype(BF16)
        d['kt'] = (d['k'] * d['etail']).astype(BF16)
        d['attn'] = attn

    states = [s_ref[h] for h in heads]
    for c in range(nchunks):
        rows = slice(c * CHUNK, (c + 1) * CHUNK)
        wqs = [_dot(it[(c, h)]['wqd'], states[h].astype(BF16)) for h in heads]
        v_news = [it[(c, h)]['u'] - wqs[h][:CHUNK] for h in heads]
        v_bfs = [vn.astype(BF16) for vn in v_news]
        o_cs = [wqs[h][CHUNK:] + _dot(it[(c, h)]['attn'], v_bfs[h]) for h in heads]
        states = [states[h] * jnp.exp(it[(c, h)]['g_end']) + lax.dot_general(
            it[(c, h)]['kt'], v_bfs[h], (((0,), (0,)), ((), ())), preferred_element_type=F32) for h in heads]
        for h in heads:
            hl = slice(h * GDN_DV, (h + 1) * GDN_DV)
            o_c = o_cs[h]
            o_n = o_c * lax.rsqrt(jnp.mean(o_c * o_c, axis=-1, keepdims=True) + NORM_EPS) * nw
            o_ref[rows, hl] = (o_n * _silu(z_ref[rows, hl])).astype(BF16)
    for h in heads:
        s_ref[h] = states[h]


def _gdn_call(p_main, p_small, p_small_t, conv_w, rowp, colp, norm_w):
    l = p_main.shape[0]
    r = GDN_BLOCK
    wide = lambda base: pl.BlockSpec((r, GDN_KDIM), lambda b: (b, base // GDN_KDIM))
    full = lambda *shape: pl.BlockSpec(shape, lambda b: (0,) * len(shape))
    return pl.pallas_call(
        _gdn_kernel,
        grid=(l // r,),
        in_specs=[wide(COL_Q), wide(COL_K), wide(COL_V), wide(COL_Z),
                  pl.BlockSpec((r, LANES), lambda b: (b, 0)),
                  pl.BlockSpec((SMALL_ROWS, r), lambda b: (0, b)),
                  full(CONV_WIDTH, 2 * GDN_KDIM + GDN_VDIM),
                  full(2, LANES), full(SMALL_ROWS, 2), full(1, GDN_DV)],
        out_specs=pl.BlockSpec((r, GDN_VDIM), lambda b: (b, 0)),
        out_shape=jax.ShapeDtypeStruct((l, GDN_VDIM), BF16),
        scratch_shapes=[pltpu.VMEM((GDN_HEADS, GDN_DK, GDN_DV), F32),
                        pltpu.VMEM((SUBLANES, GDN_KDIM), F32),
                        pltpu.VMEM((SUBLANES, GDN_KDIM), F32),
                        pltpu.VMEM((SUBLANES, GDN_VDIM), F32)],
        compiler_params=_params(("arbitrary",)),
        name="gated_deltanet",
    )(p_main, p_main, p_main, p_main, p_small, p_small_t, conv_w, rowp, colp, norm_w)


def _merge_kernel(ys_ref, yg_ref, gs_ref, gg_ref, x_ref, wus_ref, wug_ref, wo_ref,
                  gm_ref, g1_ref, b1_ref, sc_ref, sh_ref, xo_ref, ho_ref):
    a = _dot(ys_ref[...], wus_ref[...])
    b = _dot(yg_ref[...], wug_ref[...])
    m = _sigmoid(gs_ref[...]) * a + _sigmoid(gg_ref[...]) * b
    y = _dot(m.astype(BF16), wo_ref[...])
    xn = _ln(DEEPNORM_ALPHA * x_ref[...] + gm_ref[...] * y) * g1_ref[...] + b1_ref[...]
    xo_ref[...] = xn
    ho_ref[...] = (_ln(xn) * (1.0 + sc_ref[...]) + sh_ref[...]).astype(BF16)


def _merge_call(layer, ys, yg, p_main, x, w_us, w_ug, w_out, g_m, ln_g, ln_b, sc_f, sh_f):
    l, d = x.shape
    tm = 256
    vec = pl.BlockSpec((1, d), lambda i: (0, 0))
    once = lambda shape: pl.BlockSpec((None,) + shape, lambda i: (layer, 0, 0), pipeline_mode=pl.Buffered(1))
    return pl.pallas_call(
        _merge_kernel,
        grid=(l // tm,),
        in_specs=[pl.BlockSpec((tm, SSM_WIDTH), lambda i: (i, 0)),
                  pl.BlockSpec((tm, GDN_VDIM), lambda i: (i, 0)),
                  pl.BlockSpec((tm, d), lambda i: (i, COL_GATE_S // d)),
                  pl.BlockSpec((tm, d), lambda i: (i, COL_GATE_G // d)),
                  pl.BlockSpec((tm, d), lambda i: (i, 0)),
                  once((SSM_WIDTH, d)), once((GDN_VDIM, d)), once((d, d)),
                  vec, vec, vec, vec, vec],
        out_specs=[pl.BlockSpec((tm, d), lambda i: (i, 0)),
                   pl.BlockSpec((tm, d), lambda i: (i, 0))],
        out_shape=[jax.ShapeDtypeStruct((l, d), F32), jax.ShapeDtypeStruct((l, d), BF16)],
        compiler_params=_params(("arbitrary",)),
        name="merge_out_ln1",
    )(ys, yg, p_main, p_main, x, w_us, w_ug, w_out, g_m, ln_g, ln_b, sc_f, sh_f)


def _ffn_kernel(with_next, h_ref, wg_ref, wu_ref, wo_ref, x_ref, gf_ref, g2_ref, b2_ref, sc_ref, sh_ref,
                xo_ref, *rest):
    ho_ref, acc_ref = rest if with_next else (None, rest[0])
    j = pl.program_id(1)

    @pl.when(j == 0)
    def _():
        acc_ref[...] = jnp.zeros_like(acc_ref)

    h = h_ref[...]
    gate = _dot(h, wg_ref[...])
    up = _dot(h, wu_ref[...])
    acc_ref[...] += _dot((_silu(gate) * up).astype(BF16), wo_ref[...])

    @pl.when(j == pl.num_programs(1) - 1)
    def _():
        xn = _ln(DEEPNORM_ALPHA * x_ref[...] + gf_ref[...] * acc_ref[...]) * g2_ref[...] + b2_ref[...]
        xo_ref[...] = xn
        if with_next:
            ho_ref[...] = (_ln(xn) * (1.0 + sc_ref[...]) + sh_ref[...]).astype(BF16)


def _ffn_call(layer, h, w_in, w_out, x, g_f, ln_g, ln_b, sc_n, sh_n, with_next):
    l, d = x.shape
    f = w_out.shape[1]
    tm, th = 512, 512
    nj = f // th
    vec = pl.BlockSpec((1, d), lambda i, j: (0, 0))
    row_tile = pl.BlockSpec((tm, d), lambda i, j: (i, 0))
    out_specs = [row_tile] + ([row_tile] if with_next else [])
    out_shape = [jax.ShapeDtypeStruct((l, d), F32)] + ([jax.ShapeDtypeStruct((l, d), BF16)] if with_next else [])
    outs = pl.pallas_call(
        functools.partial(_ffn_kernel, with_next),
        grid=(l // tm, nj),
        in_specs=[row_tile,
                  pl.BlockSpec((None, d, th), lambda i, j: (layer, 0, j)),
                  pl.BlockSpec((None, d, th), lambda i, j: (layer, 0, nj + j)),
                  pl.BlockSpec((None, th, d), lambda i, j: (layer, j, 0)),
                  row_tile,
                  vec, vec, vec, vec, vec],
        out_specs=out_specs,
        out_shape=out_shape,
        scratch_shapes=[pltpu.VMEM((tm, d), F32)],
        compiler_params=_params(("arbitrary", "arbitrary")),
        name="ffn_ln2",
    )(h, w_in, w_in, w_out, x, g_f, ln_g, ln_b, sc_n, sh_n)
    return (outs[0], outs[1]) if with_next else (outs[0], None)


def _split_w_in(w):
    o_gates = SSM_WIDTH + 2 * GDN_KDIM + 2 * GDN_VDIM + SMALL_ROWS
    wt = jnp.swapaxes(w.astype(BF16), 1, 2)
    return wt[:, o_gates:o_gates + 2 * D_MODEL, :], wt


def kernel(x, c, w_ada, b_ada, w_in, ssm_lam_re, ssm_lam_im, ssm_log_dt, ssm_b_re, ssm_b_im, ssm_c_re, ssm_c_im, ssm_d, ssm_w_glu, ssm_b_glu, gdn_conv_w, gdn_a_log, gdn_dt_bias, gdn_norm_w, w_up_ssm, w_up_gdn, w_mix_out, ln1_g, ln1_b, ffn_w_in, ffn_w_out, ln2_g, ln2_b):
    bsz, seq, d = x.shape
    assert bsz == 1 and d == D_MODEL and seq % 1024 == 0
    xs = x.reshape(seq, d).astype(F32)
    mod = _ada_call(c.astype(F32), w_ada.astype(F32), b_ada.astype(F32))
    sh_m, sc_m, g_m, sh_f, sc_f, g_f = [mod[:, :, i * d:(i + 1) * d] for i in range(6)]

    wt_gates, wt_all = _split_w_in(w_in)
    w_glu_b = ssm_w_glu.astype(BF16)
    w_us_b, w_ug_b, w_out_b = w_up_ssm.astype(BF16), w_up_gdn.astype(BF16), w_mix_out.astype(BF16)
    ffn_in_b, ffn_out_b = ffn_w_in.astype(BF16), ffn_w_out.astype(BF16)

    h = _lnmod_call(xs, sc_m[0], sh_m[0])
    for l in range(DEPTH):
        p_main, p_small, p_small_t = _proj_call(l, h, wt_gates, wt_all)

        bmat, cmat, a_tab, aseg_tab = _s5_tables(ssm_lam_re[l], ssm_lam_im[l], ssm_log_dt[l], ssm_b_re[l],
                                                 ssm_b_im[l], ssm_c_re[l], ssm_c_im[l])
        ys = _s5_call(l, p_main, bmat, cmat, a_tab, aseg_tab, ssm_d[l].astype(F32).reshape(1, -1),
                      w_glu_b, ssm_b_glu[l].astype(F32).reshape(1, -1))

        neg_a = -jnp.exp(gdn_a_log[l].astype(F32))
        dtb = gdn_dt_bias[l].astype(F32)
        zeros = jnp.zeros((GDN_HEADS,), F32)
        decay_a = jnp.concatenate([zeros, neg_a])
        decay_b = jnp.concatenate([zeros, dtb])
        rowp = jnp.pad(jnp.stack([decay_a, decay_b]), ((0, 0), (0, LANES - SMALL_ROWS)))
        colp = jnp.stack([decay_a, decay_b], axis=1)
        yg = _gdn_call(p_main, p_small, p_small_t, gdn_conv_w[l].astype(F32), rowp, colp,
                       gdn_norm_w[l].astype(F32).reshape(1, -1))

        xs, h2 = _merge_call(l, ys, yg, p_main, xs, w_us_b, w_ug_b, w_out_b, g_m[l],
                             ln1_g[l].astype(F32).reshape(1, -1), ln1_b[l].astype(F32).reshape(1, -1),
                             sc_f[l], sh_f[l])
        nxt = min(l + 1, DEPTH - 1)
        xs, h = _ffn_call(l, h2, ffn_in_b, ffn_out_b, xs, g_f[l],
                          ln2_g[l].astype(F32).reshape(1, -1), ln2_b[l].astype(F32).reshape(1, -1),
                          sc_m[nxt], sh_m[nxt], with_next=l + 1 < DEPTH)
    return xs.reshape(bsz, seq, d)
```
